```python
import functools
import jax, jax.numpy as jnp
from jax import lax
import numpy as np

D_MODEL = 1024
BATCH = 2
SEQ = 8192
DEPTH = 4
DEC_BATCH = 128
DEC_SEQ = 4
PAST_LEN = 2048
PAGE_SIZE = 128

N_HEADS = 8
HEAD_DIM = 64
D_ATTN = N_HEADS * HEAD_DIM
FORGET_BIAS = 4.0
Q_BLOCK = 128
POOL_WINDOWS = (2, 4, 8, 16)
N_POOL_GROUPS = len(POOL_WINDOWS)
D_POOL = D_MODEL // 2
POOL_GC = D_POOL // N_POOL_GROUPS
POOL_HIST = max(POOL_WINDOWS) - 1
EPS = 1e-6

OFF_Q = 0
OFF_K = OFF_Q + D_ATTN
OFF_V = OFF_K + D_ATTN
OFF_F = OFF_V + D_ATTN
OFF_GA = OFF_F + N_HEADS
OFF_Z = OFF_GA + D_ATTN
OFF_GP = OFF_Z + D_POOL
OFF_M = OFF_GP + D_POOL
D_IN = OFF_M + 2 * D_MODEL

kernel_name = 'fox_pool_hybrid_step'


def _rms_norm(x, g):
    xf = x.astype(jnp.float32)
    y = xf * lax.rsqrt(jnp.mean(xf * xf, axis=-1, keepdims=True) + EPS)
    return (y * g.astype(jnp.float32)).astype(x.dtype)


def _fox_prompt(q, k, v, logf):
    B, T = q.shape[0], q.shape[1]
    n_blk = T // Q_BLOCK
    F = jnp.cumsum(logf, axis=1)
    Fk = jnp.transpose(F, (0, 2, 1))
    qb = jnp.transpose(q.reshape(B, n_blk, Q_BLOCK, N_HEADS, HEAD_DIM), (1, 0, 2, 3, 4))
    Fq = jnp.transpose(F.reshape(B, n_blk, Q_BLOCK, N_HEADS), (1, 0, 3, 2))
    key_pos = jnp.arange(T)

    def block(args):
        i, qi, fi = args
        s = jnp.einsum('bqhd,bkhd->bhqk', qi, k).astype(jnp.float32) * (HEAD_DIM ** -0.5)
        s = s + fi[..., None] - Fk[:, :, None, :]
        q_pos = i * Q_BLOCK + jnp.arange(Q_BLOCK)
        s = jnp.where(key_pos[None, :] <= q_pos[:, None], s, -jnp.inf)
        pr = jax.nn.softmax(s, axis=-1).astype(v.dtype)
        return jnp.einsum('bhqk,bkhd->bqhd', pr, v)

    out = lax.map(block, (jnp.arange(n_blk), qb, Fq))
    return jnp.transpose(out, (1, 0, 2, 3, 4)).reshape(B, T, D_ATTN)


def _fox_sample(q, k, v, logf, ck, cv, cf, page_table):
    DB, S = q.shape[0], q.shape[1]
    k_past = ck[page_table].reshape(DB, -1, N_HEADS, HEAD_DIM)
    v_past = cv[page_table].reshape(DB, -1, N_HEADS, HEAD_DIM)
    f_past = cf[page_table].reshape(DB, -1, N_HEADS)
    P = k_past.shape[1]
    k_all = jnp.concatenate([k_past, k], axis=1)
    v_all = jnp.concatenate([v_past, v], axis=1)
    F = jnp.cumsum(jnp.concatenate([f_past.astype(jnp.float32), logf], axis=1), axis=1)
    Fk = jnp.transpose(F, (0, 2, 1))
    Fq = Fk[:, :, P:]
    s = jnp.einsum('bqhd,bkhd->bhqk', q, k_all).astype(jnp.float32) * (HEAD_DIM ** -0.5)
    s = s + Fq[..., None] - Fk[:, :, None, :]
    q_pos = P + jnp.arange(S)
    key_pos = jnp.arange(P + S)
    s = jnp.where(key_pos[None, :] <= q_pos[:, None], s, -jnp.inf)
    pr = jax.nn.softmax(s, axis=-1).astype(v.dtype)
    return jnp.einsum('bhqk,bkhd->bqhd', pr, v_all).reshape(DB, S, D_ATTN)


def _pool_mix(z, z_prev, pos0, w_grp, ls):
    B, T, _ = z.shape
    z_ext = jnp.concatenate([z_prev, z], axis=1).astype(jnp.float32)
    S0 = jnp.concatenate([jnp.zeros((B, 1, D_POOL), jnp.float32), jnp.cumsum(z_ext, axis=1)], axis=1)
    end = S0[:, POOL_HIST + 1:]
    z_cur = z_ext[:, POOL_HIST:]
    pos = pos0 + jnp.arange(T)
    diffs = []
    for g, w in enumerate(POOL_WINDOWS):
        lo, hi = g * POOL_GC, (g + 1) * POOL_GC
        start = S0[:, POOL_HIST + 1 - w:POOL_HIST + 1 - w + T, lo:hi]
        cnt = jnp.minimum(pos + 1, w).astype(jnp.float32)[None, :, None]
        diffs.append((end[..., lo:hi] - start) / cnt - z_cur[..., lo:hi])
    d = jnp.stack(diffs, axis=2)
    y = jnp.einsum('btgc,gcd->btgd', d, w_grp.astype(jnp.float32)).reshape(B, T, D_POOL)
    y = y * ls.astype(jnp.float32)
    return y.astype(z.dtype), z_ext[:, -POOL_HIST:].astype(z.dtype)


def _layer(x, c, pos0, attend, pool_prev, norm_g, w_ada, b_ada, w_in, b_f, q_norm_g, k_norm_g,
           w_pool_grp, pool_scale, w_br_a, w_br_p, w_out):
    B, T, _ = x.shape
    mod = jax.nn.silu(c) @ w_ada + b_ada
    shift, scale, gate = jnp.split(mod, 3, axis=-1)
    u = _rms_norm(x, norm_g) * (1 + scale[:, None, :]) + shift[:, None, :]
    p = u @ w_in
    q = _rms_norm(p[..., OFF_Q:OFF_K].reshape(B, T, N_HEADS, HEAD_DIM), q_norm_g)
    k = _rms_norm(p[..., OFF_K:OFF_V].reshape(B, T, N_HEADS, HEAD_DIM), k_norm_g)
    v = p[..., OFF_V:OFF_F].reshape(B, T, N_HEADS, HEAD_DIM)
    logf = jax.nn.log_sigmoid((p[..., OFF_F:OFF_GA] + b_f).astype(jnp.float32))
    g_attn = p[..., OFF_GA:OFF_Z]
    z = p[..., OFF_Z:OFF_GP]
    g_pool = p[..., OFF_GP:OFF_M]
    m_a, m_p = jnp.split(p[..., OFF_M:], 2, axis=-1)
    o = attend(q, k, v, logf)
    y_pool, pool_new = _pool_mix(z, pool_prev, pos0, w_pool_grp, pool_scale)
    br_a = (o * jax.nn.silu(g_attn)) @ w_br_a
    br_p = (y_pool * jax.nn.silu(g_pool)) @ w_br_p
    merged = jax.nn.sigmoid(m_a) * br_a + jax.nn.sigmoid(m_p) * br_p
    y = x + gate[:, None, :] * (merged @ w_out)
    return y, k, v, logf.astype(x.dtype), pool_new


def setup_inputs(seed: int = 0) -> dict:
    key = jax.random.key(seed)
    ks = jax.random.split(key, 24)
    nrm = jax.random.normal
    n_pages = PAST_LEN // PAGE_SIZE
    n_used = DEC_BATCH * n_pages
    n_pool = n_used + max(1, n_used // 4)
    x_prompt = nrm(ks[0], (BATCH, SEQ, D_MODEL), jnp.float32)
    x_sample = nrm(ks[1], (DEC_BATCH, DEC_SEQ, D_MODEL), jnp.float32)
    cache_k = nrm(ks[2], (DEPTH, n_pool, PAGE_SIZE, N_HEADS, HEAD_DIM), jnp.float32)
    cache_v = nrm(ks[3], (DEPTH, n_pool, PAGE_SIZE, N_HEADS, HEAD_DIM), jnp.float32)
    cache_logf = jax.nn.log_sigmoid(FORGET_BIAS + 0.5 * nrm(ks[4], (DEPTH, n_pool, PAGE_SIZE, N_HEADS), jnp.float32))
    state_pool = nrm(ks[5], (DEPTH, DEC_BATCH, POOL_HIST, D_POOL), jnp.float32)
    page_table = jax.random.permutation(ks[6], n_pool)[:n_used].reshape(DEC_BATCH, n_pages).astype(jnp.int32)
    c_prompt = nrm(ks[7], (BATCH, D_MODEL), jnp.float32)
    c_sample = nrm(ks[8], (DEC_BATCH, D_MODEL), jnp.float32)
    norm_g = 1.0 + 0.02 * nrm(ks[9], (DEPTH, D_MODEL), jnp.float32)
    w_ada = 0.5 * D_MODEL ** -0.5 * nrm(ks[10], (DEPTH, D_MODEL, 3 * D_MODEL), jnp.float32)
    b_ada = 0.02 * nrm(ks[11], (DEPTH, 3 * D_MODEL), jnp.float32)
    col_scale = jnp.ones((D_IN,), jnp.float32).at[OFF_F:OFF_GA].set(0.5)
    w_in = D_MODEL ** -0.5 * nrm(ks[12], (DEPTH, D_MODEL, D_IN), jnp.float32) * col_scale
    b_f = FORGET_BIAS + 0.5 * nrm(ks[13], (DEPTH, N_HEADS), jnp.float32)
    q_norm_g = 1.0 + 0.02 * nrm(ks[14], (DEPTH, HEAD_DIM), jnp.float32)
    k_norm_g = 1.0 + 0.02 * nrm(ks[15], (DEPTH, HEAD_DIM), jnp.float32)
    w_pool_grp = POOL_GC ** -0.5 * nrm(ks[16], (DEPTH, N_POOL_GROUPS, POOL_GC, POOL_GC), jnp.float32)
    pool_scale = 1.0 + 0.1 * nrm(ks[17], (DEPTH, D_POOL), jnp.float32)
    w_br_a = D_ATTN ** -0.5 * nrm(ks[18], (DEPTH, D_ATTN, D_MODEL), jnp.float32)
    w_br_p = D_POOL ** -0.5 * nrm(ks[19], (DEPTH, D_POOL, D_MODEL), jnp.float32)
    w_out = D_MODEL ** -0.5 * nrm(ks[20], (DEPTH, D_MODEL, D_MODEL), jnp.float32)
    return {'x_prompt': x_prompt, 'x_sample': x_sample, 'cache_k': cache_k, 'cache_v': cache_v,
            'cache_logf': cache_logf, 'state_pool': state_pool, 'page_table': page_table,
            'c_prompt': c_prompt, 'c_sample': c_sample, 'norm_g': norm_g, 'w_ada': w_ada,
            'b_ada': b_ada, 'w_in': w_in, 'b_f': b_f, 'q_norm_g': q_norm_g, 'k_norm_g': k_norm_g,
            'w_pool_grp': w_pool_grp, 'pool_scale': pool_scale, 'w_br_a': w_br_a,
            'w_br_p': w_br_p, 'w_out': w_out}


def reference(x_prompt, x_sample, cache_k, cache_v, cache_logf, state_pool, page_table,
              c_prompt, c_sample, norm_g, w_ada, b_ada, w_in, b_f, q_norm_g, k_norm_g,
              w_pool_grp, pool_scale, w_br_a, w_br_p, w_out):
    hp = x_prompt
    hs = x_sample
    kp, vp, fp, pp = [], [], [], []
    ksm, vsm, fsm, psm = [], [], [], []
    pool_zero = jnp.zeros((x_prompt.shape[0], POOL_HIST, D_POOL), x_prompt.dtype)
    for l in range(DEPTH):
        wl = (norm_g[l], w_ada[l], b_ada[l], w_in[l], b_f[l], q_norm_g[l], k_norm_g[l],
              w_pool_grp[l], pool_scale[l], w_br_a[l], w_br_p[l], w_out[l])
        hp, k1, v1, f1, p1 = _layer(hp, c_prompt, 0, _fox_prompt, pool_zero, *wl)
        attend_s = functools.partial(_fox_sample, ck=cache_k[l], cv=cache_v[l], cf=cache_logf[l],
                                     page_table=page_table)
        hs, k2, v2, f2, p2 = _layer(hs, c_sample, PAST_LEN, attend_s, state_pool[l], *wl)
        kp.append(k1); vp.append(v1); fp.append(f1); pp.append(p1)
        ksm.append(k2); vsm.append(v2); fsm.append(f2); psm.append(p2)
    return (hp, hs, jnp.stack(kp), jnp.stack(vp), jnp.stack(fp), jnp.stack(pp),
            jnp.stack(ksm), jnp.stack(vsm), jnp.stack(fsm), jnp.stack(psm))
```

```python
import functools
import math

import jax
import jax.numpy as jnp
from jax import lax
from jax.experimental import pallas as pl
from jax.experimental.pallas import tpu as pltpu

N_HEADS = 8
HEAD_DIM = 64
D_ATTN = N_HEADS * HEAD_DIM
POOL_WINDOWS = (2, 4, 8, 16)
POOL_GC = 128
D_POOL = POOL_GC * len(POOL_WINDOWS)
POOL_HIST = max(POOL_WINDOWS) - 1
POOL_SLOTS = POOL_HIST + 1
EPS = 1e-6
LOG2E = math.log2(math.e)
LANES = 128
F_PAD = LANES
Q_ROWS = 16
VMEM_LIMIT = 56 * 1024 * 1024

C_Q, C_K, C_V, C_GA, C_Z, C_GP = 0, 512, 1024, 1536, 2048, 2560
C_MA = 3072


def _split3(x):
    hi = x.astype(jnp.bfloat16)
    r1 = x - hi.astype(jnp.float32)
    mid = r1.astype(jnp.bfloat16)
    lo = (r1 - mid.astype(jnp.float32)).astype(jnp.bfloat16)
    return hi, mid, lo


def _dot(a, b):
    return jnp.dot(a, b, preferred_element_type=jnp.float32)


def _dot_exact_rhs(sel, x):
    hi, mid, lo = _split3(x)
    return _dot(sel, hi) + _dot(sel, mid) + _dot(sel, lo)


def _dot_exact_lhs(x, sel):
    hi, mid, lo = _split3(x)
    return _dot(hi, sel) + _dot(mid, sel) + _dot(lo, sel)


def _div_pow2(x, n):
    assert n & (n - 1) == 0
    return lax.shift_right_logical(x, n.bit_length() - 1)


def _mod_pow2(x, n):
    assert n & (n - 1) == 0
    return lax.bitwise_and(x, n - 1)


def _silu(x):
    return x * jax.nn.sigmoid(x)


def _log_sigmoid(x):
    return jnp.minimum(x, 0.0) - jnp.log1p(jnp.exp(-jnp.abs(x)))


def _mod_kernel(c_ref, w_ref, b_ref, o_ref):
    a_hi, a_mid, a_lo = _split3(_silu(c_ref[...]))
    w_hi, w_mid, w_lo = _split3(w_ref[...])
    acc = _dot(a_hi, w_hi) + (_dot(a_hi, w_mid) + _dot(a_mid, w_hi))
    acc = acc + (_dot(a_hi, w_lo) + _dot(a_lo, w_hi) + _dot(a_mid, w_mid))
    o_ref[...] = acc + b_ref[...]


def _modulation(c_all, w_ada_bf, b_ada):
    depth, d, d3 = w_ada_bf.shape
    mc = c_all.shape[0]
    tn = 512
    return pl.pallas_call(
        _mod_kernel,
        grid=(depth, d3 // tn),
        in_specs=[pl.BlockSpec((mc, d), lambda l, j: (0, 0)),
                  pl.BlockSpec((None, d, tn), lambda l, j: (l, 0, j)),
                  pl.BlockSpec((None, 1, tn), lambda l, j: (l, 0, j))],
        out_specs=pl.BlockSpec((None, mc, tn), lambda l, j: (l, 0, j)),
        out_shape=jax.ShapeDtypeStruct((depth, mc, d3), jnp.float32),
        name="adaln_mod",
    )(c_all, w_ada_bf, b_ada.reshape(depth, 1, d3))


def _inproj_kernel(x_ref, scale_ref, shift_ref, g_ref, w_ref, bf_ref, qg_ref, kg_ref, ones_ref, tri_ref,
                   hist_ref, wpool_ref, pscale_ref, wbrp_ref,
                   q_ref, k32_ref, kb_ref, v32_ref, vb_ref, logf_ref, fs_ref, ga_ref, sma_ref, gp_ref, tail_ref,
                   zext_scr, fcarry_scr, *, tm, tstride, tiles_per_seq, prompt):
    hist_rows = POOL_SLOTS * tstride
    i = pl.program_id(0)
    tile_in_seq = lax.rem(i, tiles_per_seq)
    first = tile_in_seq == 0

    @pl.when(first)
    def _():
        zext_scr[0:hist_rows, :] = hist_ref[...]
        fcarry_scr[...] = jnp.zeros_like(fcarry_scr)

    x = x_ref[...]
    ms = jnp.mean(x * x, axis=-1, keepdims=True)
    xn = x * lax.rsqrt(ms + EPS) * g_ref[...]
    u = (xn * (1.0 + scale_ref[...]) + shift_ref[...]).astype(jnp.bfloat16)

    def proj(c0, width):
        return _dot(u, w_ref[:, c0:c0 + width])

    def head_norm(p, gain_ref):
        ssum = _dot_exact_lhs(p * p, ones_ref[...])
        return p * lax.rsqrt(ssum * (1.0 / HEAD_DIM) + EPS) * gain_ref[...]

    qn = head_norm(proj(C_Q, D_ATTN), qg_ref)
    q_ref[...] = (qn * (HEAD_DIM ** -0.5 * LOG2E)).astype(jnp.bfloat16)
    kn = head_norm(proj(C_K, D_ATTN), kg_ref)
    k32_ref[...] = kn
    kb_ref[...] = kn.astype(jnp.bfloat16)
    pv = proj(C_V, D_ATTN)
    v32_ref[...] = pv
    vb_ref[...] = pv.astype(jnp.bfloat16)

    c_f = w_ref.shape[1] - F_PAD
    logf = _log_sigmoid(proj(c_f, F_PAD) + bf_ref[...])
    logf_ref[...] = logf[:, :N_HEADS]
    if prompt:
        fc = _dot_exact_rhs(tri_ref[...], logf) + fcarry_scr[...]
        fcarry_scr[...] = fc[tm - 1:tm, :]
        fs_ref[...] = (fc * LOG2E)[:, :N_HEADS]
    else:
        fs_ref[...] = jnp.zeros_like(fs_ref)

    ga_ref[...] = _silu(proj(C_GA, D_ATTN)).astype(jnp.bfloat16)

    zext_scr[hist_rows:hist_rows + tm, :] = proj(C_Z, D_POOL)
    if prompt:
        pos = tile_in_seq * tm + lax.broadcasted_iota(jnp.int32, (tm, 1), 0)
    ys = []
    for g, w in enumerate(POOL_WINDOWS):
        col = zext_scr[:, g * POOL_GC:(g + 1) * POOL_GC]
        s = col
        span = 1
        while span < w:
            s = s + pltpu.roll(s, span * tstride, axis=0)
            span *= 2
        if prompt:
            cnt = jnp.minimum(pos + 1, w).astype(jnp.float32)
        else:
            cnt = jnp.float32(w)
        diff = s[hist_rows:, :] / cnt - col[hist_rows:, :]
        ys.append(_dot(diff.astype(jnp.bfloat16), wpool_ref[g]))
    y_pool = jnp.concatenate(ys, axis=1) * pscale_ref[...]
    tail = zext_scr[tm:tm + hist_rows, :]
    tail_ref[...] = tail
    if tiles_per_seq > 1:
        zext_scr[0:hist_rows, :] = tail

    h_p = (y_pool * _silu(proj(C_GP, D_POOL))).astype(jnp.bfloat16)
    d_model = x.shape[1]
    half = d_model // 2
    for c in range(2):
        br_p = _dot(h_p, wbrp_ref[:, c * half:(c + 1) * half])
        m_p = proj(C_MA + d_model + c * half, half)
        gp_ref[:, c * half:(c + 1) * half] = (jax.nn.sigmoid(m_p) * br_p).astype(jnp.bfloat16)
        m_a = proj(C_MA + c * half, half)
        sma_ref[:, c * half:(c + 1) * half] = jax.nn.sigmoid(m_a).astype(jnp.bfloat16)


def _inproj(x, scale, shift, hist, lw, consts, *, tm, tstride, tiles_per_seq, prompt):
    rows, d = x.shape
    n_tiles = rows // tm
    hist_rows = POOL_SLOTS * tstride
    w_in = lw["w_in"]
    bf, f32 = jnp.bfloat16, jnp.float32

    def full(a):
        return pl.BlockSpec(a.shape, lambda i, nd=a.ndim: (0,) * nd)

    if scale.ndim == 3:
        mod_spec = pl.BlockSpec((None, 1, d), lambda i: (i // tiles_per_seq, 0, 0))
    else:
        mod_spec = pl.BlockSpec((tm, d), lambda i: (i, 0))

    def rowblk(width):
        return pl.BlockSpec((tm, width), lambda i: (i, 0))

    n_seq = n_tiles // tiles_per_seq
    out_shape = (
        jax.ShapeDtypeStruct((rows, D_ATTN), bf),
        jax.ShapeDtypeStruct((rows, D_ATTN), f32),
        jax.ShapeDtypeStruct((rows, D_ATTN), bf),
        jax.ShapeDtypeStruct((rows, D_ATTN), f32),
        jax.ShapeDtypeStruct((rows, D_ATTN), bf),
        jax.ShapeDtypeStruct((rows, N_HEADS), f32),
        jax.ShapeDtypeStruct((rows, N_HEADS), f32),
        jax.ShapeDtypeStruct((rows, D_ATTN), bf),
        jax.ShapeDtypeStruct((rows, d), bf),
        jax.ShapeDtypeStruct((rows, d), bf),
        jax.ShapeDtypeStruct((n_seq, hist_rows, D_POOL), f32),
    )
    out_specs = (rowblk(D_ATTN), rowblk(D_ATTN), rowblk(D_ATTN), rowblk(D_ATTN), rowblk(D_ATTN),
                 rowblk(N_HEADS), rowblk(N_HEADS), rowblk(D_ATTN), rowblk(d), rowblk(d),
                 pl.BlockSpec((None, hist_rows, D_POOL), lambda i: (i // tiles_per_seq, 0, 0)))
    kern = functools.partial(_inproj_kernel, tm=tm, tstride=tstride, tiles_per_seq=tiles_per_seq,
                             prompt=prompt)
    return pl.pallas_call(
        kern,
        grid=(n_tiles,),
        in_specs=[rowblk(d), mod_spec, mod_spec, full(lw["norm_g"]), full(w_in), full(lw["b_f"]),
                  full(lw["q_gain"]), full(lw["k_gain"]), full(consts["ones_bd"]), full(consts["tri"]),
                  full(hist), full(lw["w_pool"]), full(lw["pool_scale"]), full(lw["w_br_p"])],
        out_specs=out_specs,
        out_shape=out_shape,
        scratch_shapes=[pltpu.VMEM((hist_rows + tm, D_POOL), f32), pltpu.VMEM((1, F_PAD), f32)],
        compiler_params=pltpu.CompilerParams(dimension_semantics=("arbitrary",), vmem_limit_bytes=VMEM_LIMIT),
        name="in_proj",
    )(x, scale, shift, lw["norm_g"], w_in, lw["b_f"], lw["q_gain"], lw["k_gain"], consts["ones_bd"],
      consts["tri"], hist, lw["w_pool"], lw["pool_scale"], lw["w_br_p"])


def _attn_kernel(q_ref, k_ref, v_ref, fq_ref, fk_ref, o_ref, m_scr, l_scr, acc_scr, *, tq):
    qi = pl.program_id(2)
    hp = pl.program_id(1)
    q2 = q_ref[...]
    lane = lax.broadcasted_iota(jnp.int32, (tq, LANES), 1)
    row = lax.broadcasted_iota(jnp.int32, (tq, tq), 0)
    col = lax.broadcasted_iota(jnp.int32, (tq, tq), 1)
    reps = tq // LANES
    outs = []
    fq_all = fq_ref[...]
    head_lane = lax.broadcasted_iota(jnp.int32, fq_all.shape, 1)
    for hh in range(2):
        head = hp * 2 + hh
        in_head = (lane >= hh * HEAD_DIM) & (lane < (hh + 1) * HEAD_DIM)
        qh = jnp.where(in_head, q2, jnp.zeros_like(q2))
        fq = jnp.sum(jnp.where(head_lane == head, fq_all, 0.0), axis=1, keepdims=True)
        m_scr[...] = jnp.full_like(m_scr, -jnp.inf)
        l_scr[...] = jnp.zeros_like(l_scr)
        acc_scr[...] = jnp.zeros_like(acc_scr)

        def step(kt, masked, head=head, qh=qh, fq=fq):
            start = pl.multiple_of(kt * tq, tq)
            kb = k_ref[pl.ds(start, tq), :]
            vb = v_ref[pl.ds(start, tq), :]
            s = lax.dot_general(qh, kb, (((1,), (1,)), ((), ())), preferred_element_type=jnp.float32)
            s = s + fq - fk_ref[kt, pl.ds(head, 1), :]
            if masked:
                s = jnp.where(col <= row, s, -jnp.inf)
            m_prev = m_scr[...]
            m_next = jnp.maximum(m_prev, jnp.max(s, axis=1, keepdims=True))
            p = jnp.exp2(s - jnp.tile(m_next, (1, reps)))
            alpha = jnp.exp2(m_prev - m_next)
            l_scr[...] = alpha * l_scr[...] + jnp.sum(p, axis=1, keepdims=True)
            acc_scr[...] = alpha * acc_scr[...] + _dot(p.astype(jnp.bfloat16), vb)
            m_scr[...] = m_next

        def body(kt, carry):
            step(kt, False)
            return carry

        lax.fori_loop(0, qi, body, 0)
        step(qi, True)
        outs.append((acc_scr[...] / l_scr[...], in_head))
    o = jnp.where(outs[0][1], outs[0][0], outs[1][0])
    o_ref[...] = o.astype(o_ref.dtype)


def _prompt_attention(q, k, v, fq, fk, *, tq):
    b, t, _ = q.shape
    n_t = t // tq
    kern = functools.partial(_attn_kernel, tq=tq)
    return pl.pallas_call(
        kern,
        grid=(b, N_HEADS // 2, n_t),
        in_specs=[pl.BlockSpec((None, tq, LANES), lambda bi, hp, qi: (bi, qi, hp)),
                  pl.BlockSpec((None, t, LANES), lambda bi, hp, qi: (bi, 0, hp)),
                  pl.BlockSpec((None, t, LANES), lambda bi, hp, qi: (bi, 0, hp)),
                  pl.BlockSpec((None, tq, N_HEADS), lambda bi, hp, qi: (bi, qi, 0)),
                  pl.BlockSpec((None, n_t, N_HEADS, tq), lambda bi, hp, qi: (bi, 0, 0, 0))],
        out_specs=pl.BlockSpec((None, tq, LANES), lambda bi, hp, qi: (bi, qi, hp)),
        out_shape=jax.ShapeDtypeStruct((b, t, D_ATTN), jnp.bfloat16),
        scratch_shapes=[pltpu.VMEM((tq, LANES), jnp.float32), pltpu.VMEM((tq, LANES), jnp.float32),
                        pltpu.VMEM((tq, LANES), jnp.float32)],
        compiler_params=pltpu.CompilerParams(dimension_semantics=("arbitrary", "arbitrary", "arbitrary"),
                                             vmem_limit_bytes=VMEM_LIMIT),
        name="fox_prompt_attn",
    )(q, k, v, fq, fk)


def _sample_attn_kernel(pt_ref, q_ref, kn_ref, vn_ref, lfa_ref, lfb_ref, lower_ref, *rest, n_pages, s_new):
    lf_refs = rest[:n_pages]
    k_refs = rest[n_pages:2 * n_pages]
    v_refs = rest[2 * n_pages:3 * n_pages]
    o_ref, g_scr = rest[3 * n_pages:]
    f32, bf = jnp.float32, jnp.bfloat16
    nt = (((1,), (1,)), ((), ()))

    carry = jnp.zeros((N_HEADS, 1), f32)
    for p in reversed(range(n_pages)):
        lf = lf_refs[p][...]
        g_scr[p] = (_dot_exact_lhs(lf, lower_ref[...]) + carry) * LOG2E
        carry = carry + jnp.sum(lf, axis=1, keepdims=True)

    page = kn_ref.shape[-1]
    qi = lax.broadcasted_iota(jnp.int32, (Q_ROWS, page), 0)
    ti = lax.broadcasted_iota(jnp.int32, (Q_ROWS, page), 1)

    def head(h, carry_):
        qh = q_ref[h]
        c_col = jnp.sum(jnp.where(ti <= qi, lfa_ref[h], 0.0), axis=1, keepdims=True) * LOG2E
        c_row = jnp.sum(jnp.where(qi <= ti, lfb_ref[h], 0.0), axis=0, keepdims=True) * LOG2E
        s_n = _dot(qh, kn_ref[h]) + c_col - c_row
        s_n = jnp.where((ti <= qi) & (ti < s_new), s_n, -jnp.inf)
        scores = []
        m_t = s_n
        for p in range(n_pages):
            kt = k_refs[p][h].astype(bf)
            s = _dot(qh, kt) + c_col + g_scr[p, pl.ds(h, 1), :]
            scores.append(s)
            m_t = jnp.maximum(m_t, s)
        m = jnp.max(m_t, axis=1, keepdims=True)
        pr = jnp.exp2(s_n - m)
        l_t = pr
        acc = lax.dot_general(pr.astype(bf), vn_ref[h], nt, preferred_element_type=f32)
        for p in range(n_pages):
            pr = jnp.exp2(scores[p] - m)
            l_t = l_t + pr
            acc = acc + lax.dot_general(pr.astype(bf), v_refs[p][h].astype(bf), nt, preferred_element_type=f32)
        o_ref[h] = acc / jnp.sum(l_t, axis=1, keepdims=True)
        return carry_

    lax.fori_loop(0, N_HEADS, head, 0)


def _sample_attention(page_table, q16, kn_t, vn_t, lfa, lfb, lower, logf_t, cache_kt, cache_vt, layer, *, s_new):
    db, n_pages = page_table.shape
    heads, hd, page = cache_kt.shape[2:]

    def kv_spec(p):
        return pl.BlockSpec((None, None, heads, hd, page), lambda b, pt: (layer, pt[b, p], 0, 0, 0))

    def lf_spec(p):
        return pl.BlockSpec((None, None, heads, page), lambda b, pt: (layer, pt[b, p], 0, 0))

    def per_b(a):
        return pl.BlockSpec((None,) + a.shape[1:], lambda b, pt, nd=a.ndim: (b,) + (0,) * (nd - 1))

    grid_spec = pltpu.PrefetchScalarGridSpec(
        num_scalar_prefetch=1,
        grid=(db,),
        in_specs=[per_b(q16), per_b(kn_t), per_b(vn_t), per_b(lfa), per_b(lfb),
                  pl.BlockSpec(lower.shape, lambda b, pt: (0, 0))]
                 + [lf_spec(p) for p in range(n_pages)]
                 + [kv_spec(p) for p in range(n_pages)]
                 + [kv_spec(p) for p in range(n_pages)],
        out_specs=pl.BlockSpec((None, heads, Q_ROWS, hd), lambda b, pt: (b, 0, 0, 0)),
        scratch_shapes=[pltpu.VMEM((n_pages, heads, page), jnp.float32)],
    )
    kern = functools.partial(_sample_attn_kernel, n_pages=n_pages, s_new=s_new)
    return pl.pallas_call(
        kern,
        grid_spec=grid_spec,
        out_shape=jax.ShapeDtypeStruct((db, heads, Q_ROWS, hd), jnp.float32),
        compiler_params=pltpu.CompilerParams(dimension_semantics=("arbitrary",), vmem_limit_bytes=VMEM_LIMIT),
        name="fox_sample_attn",
    )(page_table, q16, kn_t, vn_t, lfa, lfb, lower,
      *([logf_t] * n_pages), *([cache_kt] * n_pages), *([cache_vt] * n_pages))


def _out_kernel(x_ref, o_ref, ga_ref, sma_ref, gp_ref, gate_ref, wbra_ref, wout_ref, y_ref):
    h_a = (o_ref[...].astype(jnp.float32) * ga_ref[...].astype(jnp.float32)).astype(jnp.bfloat16)
    br_a = _dot(h_a, wbra_ref[...])
    merged = sma_ref[...].astype(jnp.float32) * br_a + gp_ref[...].astype(jnp.float32)
    y_ref[...] = x_ref[...] + gate_ref[...] * _dot(merged.astype(jnp.bfloat16), wout_ref[...])


def _out_proj(x, o, ga, sma, gp, gate, w_br_a, w_out, *, tm, tiles_per_seq):
    rows, d = x.shape

    def rowblk(width):
        return pl.BlockSpec((tm, width), lambda i: (i, 0))

    if gate.ndim == 3:
        gate_spec = pl.BlockSpec((None, 1, d), lambda i: (i // tiles_per_seq, 0, 0))
    else:
        gate_spec = pl.BlockSpec((tm, d), lambda i: (i, 0))
    return pl.pallas_call(
        _out_kernel,
        grid=(rows // tm,),
        in_specs=[rowblk(d), rowblk(D_ATTN), rowblk(D_ATTN), rowblk(d), rowblk(d), gate_spec,
                  pl.BlockSpec(w_br_a.shape, lambda i: (0, 0)), pl.BlockSpec(w_out.shape, lambda i: (0, 0))],
        out_specs=rowblk(d),
        out_shape=jax.ShapeDtypeStruct((rows, d), jnp.float32),
        compiler_params=pltpu.CompilerParams(dimension_semantics=("arbitrary",), vmem_limit_bytes=VMEM_LIMIT),
        name="out_proj",
    )(x, o, ga, sma, gp, gate, w_br_a, w_out)


def kernel(x_prompt, x_sample, cache_k, cache_v, cache_logf, state_pool, page_table, c_prompt, c_sample,
           norm_g, w_ada, b_ada, w_in, b_f, q_norm_g, k_norm_g, w_pool_grp, pool_scale, w_br_a, w_br_p, w_out):
    f32, bf = jnp.float32, jnp.bfloat16
    b, t, d = x_prompt.shape
    db, s_new, _ = x_sample.shape
    depth = w_in.shape[0]
    page = cache_k.shape[2]
    n_pages = page_table.shape[1]
    tm = 512
    tq = 512
    assert t % tm == 0 and t % tq == 0 and db % 8 == 0 and s_new <= Q_ROWS and n_pages == page_table.shape[1]

    o_q, o_k, o_v = 0, D_ATTN, 2 * D_ATTN
    o_f = 3 * D_ATTN
    o_ga = o_f + N_HEADS
    o_z = o_ga + D_ATTN
    o_gp = o_z + D_POOL
    o_m = o_gp + D_POOL
    w_f = jnp.pad(w_in[:, :, o_f:o_ga], ((0, 0), (0, 0), (0, F_PAD - N_HEADS)))
    w_in_p = jnp.concatenate([w_in[:, :, o_q:o_f], w_in[:, :, o_ga:], w_f], axis=2).astype(bf)
    b_f_p = jnp.pad(b_f, ((0, 0), (0, F_PAD - N_HEADS))).reshape(depth, 1, F_PAD)
    q_gain = jnp.tile(q_norm_g, (1, N_HEADS)).reshape(depth, 1, D_ATTN)
    k_gain = jnp.tile(k_norm_g, (1, N_HEADS)).reshape(depth, 1, D_ATTN)
    w_pool_bf = w_pool_grp.astype(bf)
    w_br_a_bf, w_br_p_bf, w_out_bf = w_br_a.astype(bf), w_br_p.astype(bf), w_out.astype(bf)

    ri = lax.broadcasted_iota(jnp.int32, (D_ATTN, D_ATTN), 0)
    ci = lax.broadcasted_iota(jnp.int32, (D_ATTN, D_ATTN), 1)
    consts = {"ones_bd": (ri // HEAD_DIM == ci // HEAD_DIM).astype(bf)}
    rt = lax.broadcasted_iota(jnp.int32, (tm, tm), 0)
    ct = lax.broadcasted_iota(jnp.int32, (tm, tm), 1)
    consts["tri"] = (ct <= rt).astype(bf)
    rp = lax.broadcasted_iota(jnp.int32, (page, page), 0)
    cp = lax.broadcasted_iota(jnp.int32, (page, page), 1)
    lower = (rp > cp).astype(bf)
    cache_kt = jnp.transpose(cache_k, (0, 1, 3, 4, 2))
    cache_vt = jnp.transpose(cache_v, (0, 1, 3, 4, 2))
    logf_t = jnp.transpose(cache_logf, (0, 1, 3, 2))

    mc = db + 8
    c_all = jnp.concatenate([c_sample, c_prompt, jnp.zeros((mc - db - b, d), f32)], axis=0)
    mod = _modulation(c_all, w_ada, b_ada)

    hp = x_prompt.reshape(b * t, d)
    hs = jnp.transpose(x_sample, (1, 0, 2)).reshape(s_new * db, d)
    hist_zero = jnp.zeros((POOL_SLOTS, D_POOL), f32)
    tiles_per_seq = t // tm

    outs = {n: [] for n in ("kp", "vp", "fp", "pp", "ks", "vs", "fs", "ps")}
    for l in range(depth):
        lw = {"norm_g": norm_g[l].reshape(1, d), "w_in": w_in_p[l], "b_f": b_f_p[l], "q_gain": q_gain[l],
              "k_gain": k_gain[l], "w_pool": w_pool_bf[l], "pool_scale": pool_scale[l].reshape(1, D_POOL),
              "w_br_p": w_br_p_bf[l]}
        mod_s = mod[l, :db]
        mod_p = mod[l, db:db + b]
        shift_p, scale_p, gate_p = (mod_p[:, j * d:(j + 1) * d].reshape(b, 1, d) for j in range(3))
        mod_s_rows = jnp.tile(mod_s, (s_new, 1))
        shift_s, scale_s, gate_s = (mod_s_rows[:, j * d:(j + 1) * d] for j in range(3))

        (q_p, k32_p, kb_p, v32_p, vb_p, lf_p, fs_p, ga_p, sma_p, gp_p, tail_p) = _inproj(
            hp, scale_p, shift_p, hist_zero, lw, consts, tm=tm, tstride=1, tiles_per_seq=tiles_per_seq,
            prompt=True)
        fq = fs_p.reshape(b, t, N_HEADS)
        fk = jnp.transpose(fq.reshape(b, t // tq, tq, N_HEADS), (0, 1, 3, 2))
        o_p = _prompt_attention(q_p.reshape(b, t, D_ATTN), kb_p.reshape(b, t, D_ATTN),
                                vb_p.reshape(b, t, D_ATTN), fq, fk, tq=tq)
        hp = _out_proj(hp, o_p.reshape(b * t, D_ATTN), ga_p, sma_p, gp_p, gate_p, w_br_a_bf[l], w_out_bf[l],
                       tm=tm, tiles_per_seq=tiles_per_seq)
        outs["kp"].append(k32_p.reshape(b, t, N_HEADS, HEAD_DIM))
        outs["vp"].append(v32_p.reshape(b, t, N_HEADS, HEAD_DIM))
        outs["fp"].append(lf_p.reshape(b, t, N_HEADS))
        outs["pp"].append(tail_p[:, 1:, :])

        hist_s = jnp.concatenate([jnp.zeros((1, db, D_POOL), f32), jnp.transpose(state_pool[l], (1, 0, 2))],
                                 axis=0).reshape(POOL_SLOTS * db, D_POOL)
        (q_s, k32_s, kb_s, v32_s, vb_s, lf_s, _, ga_s, sma_s, gp_s, tail_s) = _inproj(
            hs, scale_s, shift_s, hist_s, lw, consts, tm=s_new * db, tstride=db, tiles_per_seq=1,
            prompt=False)
        q16 = jnp.transpose(q_s.reshape(s_new, db, N_HEADS, HEAD_DIM), (1, 2, 0, 3))
        q16 = jnp.pad(q16, ((0, 0), (0, 0), (0, Q_ROWS - s_new), (0, 0)))
        kn_t = jnp.transpose(kb_s.reshape(s_new, db, N_HEADS, HEAD_DIM), (1, 2, 3, 0))
        kn_t = jnp.pad(kn_t, ((0, 0), (0, 0), (0, 0), (0, page - s_new)))
        vn_t = jnp.transpose(vb_s.reshape(s_new, db, N_HEADS, HEAD_DIM), (1, 2, 3, 0))
        vn_t = jnp.pad(vn_t, ((0, 0), (0, 0), (0, 0), (0, page - s_new)))
        lf_bt = jnp.transpose(lf_s.reshape(s_new, db, N_HEADS), (1, 0, 2))
        lf_bh = jnp.transpose(lf_bt, (0, 2, 1))
        lfa = jnp.broadcast_to(jnp.pad(lf_bh, ((0, 0), (0, 0), (0, page - s_new)))[:, :, None, :],
                               (db, N_HEADS, Q_ROWS, page))
        lfb = jnp.broadcast_to(jnp.pad(lf_bh, ((0, 0), (0, 0), (0, Q_ROWS - s_new)))[:, :, :, None],
                               (db, N_HEADS, Q_ROWS, page))
        o_s = _sample_attention(page_table, q16, kn_t, vn_t, lfa, lfb, lower, logf_t, cache_kt, cache_vt, l,
                                s_new=s_new)
        o_s = jnp.transpose(o_s[:, :, :s_new, :], (2, 0, 1, 3)).reshape(s_new * db, D_ATTN).astype(bf)
        hs = _out_proj(hs, o_s, ga_s, sma_s, gp_s, gate_s, w_br_a_bf[l], w_out_bf[l], tm=s_new * db,
                       tiles_per_seq=1)
        outs["ks"].append(jnp.transpose(k32_s.reshape(s_new, db, N_HEADS, HEAD_DIM), (1, 0, 2, 3)))
        outs["vs"].append(jnp.transpose(v32_s.reshape(s_new, db, N_HEADS, HEAD_DIM), (1, 0, 2, 3)))
        outs["fs"].append(lf_bt)
        outs["ps"].append(jnp.transpose(tail_s.reshape(POOL_SLOTS, db, D_POOL)[1:], (1, 0, 2)))

    y_p = hp.reshape(b, t, d)
    y_s = jnp.transpose(hs.reshape(s_new, db, d), (1, 0, 2))
    return (y_p, y_s, jnp.stack(outs["kp"]), jnp.stack(outs["vp"]), jnp.stack(outs["fp"]),
            jnp.stack(outs["pp"]), jnp.stack(outs["ks"]), jnp.stack(outs["vs"]), jnp.stack(outs["fs"]),
            jnp.stack(outs["ps"]))
```

```python
import functools
import math

import jax
import jax.numpy as jnp
from jax import lax
from jax.experimental import pallas as pl
from jax.experimental.pallas import tpu as pltpu

N_HEADS = 8
HEAD_DIM = 64
D_ATTN = N_HEADS * HEAD_DIM
POOL_WINDOWS = (2, 4, 8, 16)
POOL_GC = 128
D_POOL = POOL_GC * len(POOL_WINDOWS)
POOL_HIST = max(POOL_WINDOWS) - 1
POOL_SLOTS = POOL_HIST + 1
EPS = 1e-6
LOG2E = math.log2(math.e)
LANES = 128
F_PAD = LANES
Q_ROWS = 16
VMEM_LIMIT = 56 * 1024 * 1024

C_Q, C_K, C_V, C_GA, C_Z, C_GP = 0, 512, 1024, 1536, 2048, 2560
C_MA = 3072


def _split3(x):
    hi = x.astype(jnp.bfloat16)
    r1 = x - hi.astype(jnp.float32)
    mid = r1.astype(jnp.bfloat16)
    lo = (r1 - mid.astype(jnp.float32)).astype(jnp.bfloat16)
    return hi, mid, lo


def _dot(a, b):
    return jnp.dot(a, b, preferred_element_type=jnp.float32)


def _dot_exact_rhs(sel, x):
    hi, mid, lo = _split3(x)
    return _dot(sel, hi) + _dot(sel, mid) + _dot(sel, lo)


def _dot_exact_lhs(x, sel):
    hi, mid, lo = _split3(x)
    return _dot(hi, sel) + _dot(mid, sel) + _dot(lo, sel)


def _div_pow2(x, n):
    assert n & (n - 1) == 0
    return lax.shift_right_logical(x, n.bit_length() - 1)


def _mod_pow2(x, n):
    assert n & (n - 1) == 0
    return lax.bitwise_and(x, n - 1)


def _silu(x):
    return x * jax.nn.sigmoid(x)


def _log_sigmoid(x):
    return jnp.minimum(x, 0.0) - jnp.log1p(jnp.exp(-jnp.abs(x)))


def _mod_kernel(c_ref, w_ref, b_ref, o_ref):
    a_hi, a_mid, a_lo = _split3(_silu(c_ref[...]))
    w_hi, w_mid, w_lo = _split3(w_ref[...])
    acc = _dot(a_hi, w_hi) + (_dot(a_hi, w_mid) + _dot(a_mid, w_hi))
    acc = acc + (_dot(a_hi, w_lo) + _dot(a_lo, w_hi) + _dot(a_mid, w_mid))
    o_ref[...] = acc + b_ref[...]


def _modulation(c_all, w_ada_bf, b_ada):
    depth, d, d3 = w_ada_bf.shape
    mc = c_all.shape[0]
    tn = 512
    return pl.pallas_call(
        _mod_kernel,
        grid=(depth, d3 // tn),
        in_specs=[pl.BlockSpec((mc, d), lambda l, j: (0, 0)),
                  pl.BlockSpec((None, d, tn), lambda l, j: (l, 0, j)),
                  pl.BlockSpec((None, 1, tn), lambda l, j: (l, 0, j))],
        out_specs=pl.BlockSpec((None, mc, tn), lambda l, j: (l, 0, j)),
        out_shape=jax.ShapeDtypeStruct((depth, mc, d3), jnp.float32),
        name="adaln_mod",
    )(c_all, w_ada_bf, b_ada.reshape(depth, 1, d3))


def _inproj_kernel(x_ref, scale_ref, shift_ref, g_ref, w_ref, bf_ref, qg_ref, kg_ref, place_ref, tri_ref,
                   hist_ref, wpool_ref, pscale_ref, wbrp_ref,
                   q_ref, k32_ref, kb_ref, v32_ref, vb_ref, logf_ref, ga_ref, sma_ref, gp_ref, tail_ref,
                   zext_scr, fcarry_scr, *, tm, tstride, tiles_per_seq, prompt):
    hist_rows = POOL_SLOTS * tstride
    i = pl.program_id(0)
    tile_in_seq = lax.rem(i, tiles_per_seq)
    first = tile_in_seq == 0

    @pl.when(first)
    def _():
        zext_scr[0:hist_rows, :] = hist_ref[...]
        fcarry_scr[...] = jnp.zeros_like(fcarry_scr)

    x = x_ref[...]
    ms = jnp.mean(x * x, axis=-1, keepdims=True)
    xn = x * lax.rsqrt(ms + EPS) * g_ref[...]
    u = (xn * (1.0 + scale_ref[...]) + shift_ref[...]).astype(jnp.bfloat16)

    def proj(c0, width):
        return _dot(u, w_ref[:, c0:c0 + width])

    lane = lax.broadcasted_iota(jnp.int32, (tm, LANES), 1)
    low_half = lane < HEAD_DIM

    def head_norm(p, gain_ref):
        outs = []
        for j in range(D_ATTN // LANES):
            slab = p[:, j * LANES:(j + 1) * LANES]
            sq = slab * slab
            s_lo = jnp.sum(jnp.where(low_half, sq, 0.0), axis=1, keepdims=True)
            s_hi = jnp.sum(jnp.where(low_half, 0.0, sq), axis=1, keepdims=True)
            r = jnp.where(low_half, lax.rsqrt(s_lo * (1.0 / HEAD_DIM) + EPS), lax.rsqrt(s_hi * (1.0 / HEAD_DIM) + EPS))
            outs.append(slab * r)
        return jnp.concatenate(outs, axis=1) * gain_ref[...]

    c_f = w_ref.shape[1] - F_PAD
    logf = _log_sigmoid(proj(c_f, F_PAD) + bf_ref[...])
    logf_ref[...] = logf[:, :N_HEADS]

    qn = head_norm(proj(C_Q, D_ATTN), qg_ref) * (HEAD_DIM ** -0.5 * LOG2E)
    kn = head_norm(proj(C_K, D_ATTN), kg_ref)
    k32_ref[...] = kn
    pv = proj(C_V, D_ATTN)
    v32_ref[...] = pv
    if prompt:
        fc = _dot_exact_rhs(tri_ref[...], logf) + fcarry_scr[...]
        fcarry_scr[...] = fc[tm - 1:tm, :]
        hi, mid, lo = _split3(fc * LOG2E)
        extras = _dot(jnp.concatenate([hi, mid, lo], axis=1), place_ref[...])
        for h in range(N_HEADS):
            j = h // 2
            e_h = extras[:, h * LANES:(h + 1) * LANES]

            def head_tile(x, h=h, j=j):
                slab = x[:, j * LANES:(j + 1) * LANES]
                return slab if h % 2 == 0 else pltpu.roll(slab, HEAD_DIM, axis=1)

            q_t = jnp.where(low_half, head_tile(qn), jnp.where(lane < HEAD_DIM + 3, 1.0, e_h))
            k_t = jnp.where(low_half, head_tile(kn),
                            jnp.where(lane < HEAD_DIM + 3, -e_h, jnp.where(lane < HEAD_DIM + 6, 1.0, 0.0)))
            v_t = jnp.where(low_half, head_tile(pv), jnp.where(lane == HEAD_DIM, 1.0, 0.0))
            q_ref[:, h * LANES:(h + 1) * LANES] = q_t.astype(jnp.bfloat16)
            kb_ref[:, h * LANES:(h + 1) * LANES] = k_t.astype(jnp.bfloat16)
            vb_ref[:, h * LANES:(h + 1) * LANES] = v_t.astype(jnp.bfloat16)
    else:
        q_ref[...] = qn.astype(jnp.bfloat16)
        kb_ref[...] = kn.astype(jnp.bfloat16)
        vb_ref[...] = pv.astype(jnp.bfloat16)

    ga_ref[...] = _silu(proj(C_GA, D_ATTN)).astype(jnp.bfloat16)

    zext_scr[hist_rows:hist_rows + tm, :] = proj(C_Z, D_POOL)
    if prompt:
        pos = tile_in_seq * tm + lax.broadcasted_iota(jnp.int32, (tm, 1), 0)
    ys = []
    for g, w in enumerate(POOL_WINDOWS):
        col = zext_scr[:, g * POOL_GC:(g + 1) * POOL_GC]
        s = col
        span = 1
        while span < w:
            s = s + pltpu.roll(s, span * tstride, axis=0)
            span *= 2
        if prompt:
            cnt = jnp.minimum(pos + 1, w).astype(jnp.float32)
        else:
            cnt = jnp.float32(w)
        diff = s[hist_rows:, :] / cnt - col[hist_rows:, :]
        ys.append(_dot(diff.astype(jnp.bfloat16), wpool_ref[g]))
    y_pool = jnp.concatenate(ys, axis=1) * pscale_ref[...]
    tail = zext_scr[tm:tm + hist_rows, :]
    tail_ref[...] = tail
    if tiles_per_seq > 1:
        zext_scr[0:hist_rows, :] = tail

    h_p = (y_pool * _silu(proj(C_GP, D_POOL))).astype(jnp.bfloat16)
    d_model = x.shape[1]
    half = d_model // 2
    for c in range(2):
        br_p = _dot(h_p, wbrp_ref[:, c * half:(c + 1) * half])
        m_p = proj(C_MA + d_model + c * half, half)
        gp_ref[:, c * half:(c + 1) * half] = (jax.nn.sigmoid(m_p) * br_p).astype(jnp.bfloat16)
        m_a = proj(C_MA + c * half, half)
        sma_ref[:, c * half:(c + 1) * half] = jax.nn.sigmoid(m_a).astype(jnp.bfloat16)


def _inproj(x, scale, shift, hist, lw, consts, *, tm, tstride, tiles_per_seq, prompt):
    rows, d = x.shape
    n_tiles = rows // tm
    hist_rows = POOL_SLOTS * tstride
    w_in = lw["w_in"]
    bf, f32 = jnp.bfloat16, jnp.float32

    def full(a):
        return pl.BlockSpec(a.shape, lambda i, nd=a.ndim: (0,) * nd)

    if scale.ndim == 3:
        mod_spec = pl.BlockSpec((None, 1, d), lambda i: (i // tiles_per_seq, 0, 0))
    else:
        mod_spec = pl.BlockSpec((tm, d), lambda i: (i, 0))

    def rowblk(width):
        return pl.BlockSpec((tm, width), lambda i: (i, 0))

    n_seq = n_tiles // tiles_per_seq
    w_att = N_HEADS * LANES if prompt else D_ATTN
    out_shape = (
        jax.ShapeDtypeStruct((rows, w_att), bf),
        jax.ShapeDtypeStruct((rows, D_ATTN), f32),
        jax.ShapeDtypeStruct((rows, w_att), bf),
        jax.ShapeDtypeStruct((rows, D_ATTN), f32),
        jax.ShapeDtypeStruct((rows, w_att), bf),
        jax.ShapeDtypeStruct((rows, N_HEADS), f32),
        jax.ShapeDtypeStruct((rows, D_ATTN), bf),
        jax.ShapeDtypeStruct((rows, d), bf),
        jax.ShapeDtypeStruct((rows, d), bf),
        jax.ShapeDtypeStruct((n_seq, hist_rows, D_POOL), f32),
    )
    out_specs = (rowblk(w_att), rowblk(D_ATTN), rowblk(w_att), rowblk(D_ATTN), rowblk(w_att),
                 rowblk(N_HEADS), rowblk(D_ATTN), rowblk(d), rowblk(d),
                 pl.BlockSpec((None, hist_rows, D_POOL), lambda i: (i // tiles_per_seq, 0, 0)))
    kern = functools.partial(_inproj_kernel, tm=tm, tstride=tstride, tiles_per_seq=tiles_per_seq,
                             prompt=prompt)
    return pl.pallas_call(
        kern,
        grid=(n_tiles,),
        in_specs=[rowblk(d), mod_spec, mod_spec, full(lw["norm_g"]), full(w_in), full(lw["b_f"]),
                  full(lw["q_gain"]), full(lw["k_gain"]), full(consts["place"]), full(consts["tri"]),
                  full(hist), full(lw["w_pool"]), full(lw["pool_scale"]), full(lw["w_br_p"])],
        out_specs=out_specs,
        out_shape=out_shape,
        scratch_shapes=[pltpu.VMEM((hist_rows + tm, D_POOL), f32), pltpu.VMEM((1, F_PAD), f32)],
        compiler_params=pltpu.CompilerParams(dimension_semantics=("arbitrary",), vmem_limit_bytes=VMEM_LIMIT),
        name="in_proj",
    )(x, scale, shift, lw["norm_g"], w_in, lw["b_f"], lw["q_gain"], lw["k_gain"], consts["place"],
      consts["tri"], hist, lw["w_pool"], lw["pool_scale"], lw["w_br_p"])


def _attn_kernel(q_ref, k_ref, v_ref, o_ref, m_scr, acc_scr, s_scr, *, tq):
    qi = pl.program_id(2)
    row = lax.broadcasted_iota(jnp.int32, (tq, tq), 0)
    col = lax.broadcasted_iota(jnp.int32, (tq, tq), 1)
    reps = tq // LANES
    nt = (((1,), (1,)), ((), ()))
    m_scr[...] = jnp.full_like(m_scr, -jnp.inf)
    acc_scr[...] = jnp.zeros_like(acc_scr)

    def scores(kt, hh):
        start = pl.multiple_of(kt * tq, tq)
        sl = slice(hh * LANES, (hh + 1) * LANES)
        return lax.dot_general(q_ref[:, sl], k_ref[pl.ds(start, tq), sl], nt, preferred_element_type=jnp.float32)

    def accumulate(s, kt, hh, masked):
        start = pl.multiple_of(kt * tq, tq)
        sl = slice(hh * LANES, (hh + 1) * LANES)
        if masked:
            s = jnp.where(col <= row, s, -jnp.inf)
        m_prev = m_scr[hh]
        m_next = jnp.maximum(m_prev, jnp.max(s, axis=1, keepdims=True))
        p = jnp.exp2(s - jnp.tile(m_next, (1, reps)))
        alpha = jnp.exp2(m_prev - m_next)
        acc_scr[hh] = alpha * acc_scr[hh] + _dot(p.astype(jnp.bfloat16), v_ref[pl.ds(start, tq), sl])
        m_scr[hh] = m_next

    for hh in range(2):
        s_scr[hh] = scores(0, hh)

    def body(kt, carry):
        for hh in range(2):
            s = s_scr[hh]
            s_scr[hh] = scores(kt + 1, hh)
            accumulate(s, kt, hh, False)
        return carry

    lax.fori_loop(0, qi, body, 0)
    for hh in range(2):
        accumulate(s_scr[hh], qi, hh, True)
    lane = lax.broadcasted_iota(jnp.int32, (tq, LANES), 1)
    o0 = acc_scr[0] / acc_scr[0][:, HEAD_DIM:HEAD_DIM + 1]
    o1 = acc_scr[1] / acc_scr[1][:, HEAD_DIM:HEAD_DIM + 1]
    o_ref[...] = jnp.where(lane < HEAD_DIM, o0, pltpu.roll(o1, HEAD_DIM, axis=1)).astype(o_ref.dtype)


def _prompt_attention(q, k, v, *, tq):
    b, t, _ = q.shape
    n_t = t // tq
    pair = 2 * LANES
    kern = functools.partial(_attn_kernel, tq=tq)
    return pl.pallas_call(
        kern,
        grid=(b, N_HEADS // 2, n_t),
        in_specs=[pl.BlockSpec((None, tq, pair), lambda bi, hp, qi: (bi, qi, hp)),
                  pl.BlockSpec((None, t, pair), lambda bi, hp, qi: (bi, 0, hp)),
                  pl.BlockSpec((None, t, pair), lambda bi, hp, qi: (bi, 0, hp))],
        out_specs=pl.BlockSpec((None, tq, LANES), lambda bi, hp, qi: (bi, qi, hp)),
        out_shape=jax.ShapeDtypeStruct((b, t, D_ATTN), jnp.bfloat16),
        scratch_shapes=[pltpu.VMEM((2, tq, LANES), jnp.float32), pltpu.VMEM((2, tq, LANES), jnp.float32),
                        pltpu.VMEM((2, tq, tq), jnp.float32)],
        compiler_params=pltpu.CompilerParams(dimension_semantics=("arbitrary", "arbitrary", "arbitrary"),
                                             vmem_limit_bytes=VMEM_LIMIT),
        name="fox_prompt_attn",
    )(q, k, v)


def _sample_attn_kernel(pt_ref, q_ref, kn_ref, vn_ref, lfa_ref, lfb_ref, lower_ref, *rest, n_pages, s_new):
    lf_refs = rest[:n_pages]
    k_refs = rest[n_pages:2 * n_pages]
    v_refs = rest[2 * n_pages:3 * n_pages]
    o_ref, g_scr = rest[3 * n_pages:]
    f32, bf = jnp.float32, jnp.bfloat16
    nt = (((1,), (1,)), ((), ()))

    page = kn_ref.shape[-1]
    carry = jnp.zeros((N_HEADS, 1), f32)
    for p in reversed(range(n_pages)):
        lf = lf_refs[p][...]
        g_scr[:, p * page:(p + 1) * page] = (_dot_exact_lhs(lf, lower_ref[...]) + carry) * LOG2E
        carry = carry + jnp.sum(lf, axis=1, keepdims=True)

    qi = lax.broadcasted_iota(jnp.int32, (Q_ROWS, page), 0)
    ti = lax.broadcasted_iota(jnp.int32, (Q_ROWS, page), 1)
    new_ok = (ti <= qi) & (ti < s_new)
    for h in range(N_HEADS):
        qh = q_ref[h]
        c_col = jnp.sum(jnp.where(ti <= qi, lfa_ref[h], 0.0), axis=1, keepdims=True) * LOG2E
        c_row = jnp.sum(jnp.where(qi <= ti, lfb_ref[h], 0.0), axis=0, keepdims=True) * LOG2E
        s_n = jnp.where(new_ok, _dot(qh, kn_ref[h]) + c_col - c_row, -jnp.inf)
        kt = jnp.concatenate([k_refs[p][h] for p in range(n_pages)], axis=1).astype(bf)
        vt = jnp.concatenate([v_refs[p][h] for p in range(n_pages)], axis=1).astype(bf)
        s = _dot(qh, kt) + c_col + g_scr[h:h + 1, :]
        m = jnp.maximum(jnp.max(s, axis=1, keepdims=True), jnp.max(s_n, axis=1, keepdims=True))
        pr = jnp.exp2(s - m)
        pr_n = jnp.exp2(s_n - m)
        l = jnp.sum(pr, axis=1, keepdims=True) + jnp.sum(pr_n, axis=1, keepdims=True)
        acc = (lax.dot_general(pr.astype(bf), vt, nt, preferred_element_type=f32)
               + lax.dot_general(pr_n.astype(bf), vn_ref[h], nt, preferred_element_type=f32))
        o_ref[h] = acc / l


def _sample_attention(page_table, q16, kn_t, vn_t, lfa, lfb, lower, logf_t, cache_kt, cache_vt, layer, *, s_new):
    db, n_pages = page_table.shape
    heads, hd, page = cache_kt.shape[2:]

    def kv_spec(p):
        return pl.BlockSpec((None, None, heads, hd, page), lambda b, pt: (layer, pt[b, p], 0, 0, 0))

    def lf_spec(p):
        return pl.BlockSpec((None, None, heads, page), lambda b, pt: (layer, pt[b, p], 0, 0))

    def per_b(a):
        return pl.BlockSpec((None,) + a.shape[1:], lambda b, pt, nd=a.ndim: (b,) + (0,) * (nd - 1))

    grid_spec = pltpu.PrefetchScalarGridSpec(
        num_scalar_prefetch=1,
        grid=(db,),
        in_specs=[per_b(q16), per_b(kn_t), per_b(vn_t), per_b(lfa), per_b(lfb),
                  pl.BlockSpec(lower.shape, lambda b, pt: (0, 0))]
                 + [lf_spec(p) for p in range(n_pages)]
                 + [kv_spec(p) for p in range(n_pages)]
                 + [kv_spec(p) for p in range(n_pages)],
        out_specs=pl.BlockSpec((None, heads, Q_ROWS, hd), lambda b, pt: (b, 0, 0, 0)),
        scratch_shapes=[pltpu.VMEM((heads, n_pages * page), jnp.float32)],
    )
    kern = functools.partial(_sample_attn_kernel, n_pages=n_pages, s_new=s_new)
    return pl.pallas_call(
        kern,
        grid_spec=grid_spec,
        out_shape=jax.ShapeDtypeStruct((db, heads, Q_ROWS, hd), jnp.float32),
        compiler_params=pltpu.CompilerParams(dimension_semantics=("arbitrary",), vmem_limit_bytes=VMEM_LIMIT),
        name="fox_sample_attn",
    )(page_table, q16, kn_t, vn_t, lfa, lfb, lower,
      *([logf_t] * n_pages), *([cache_kt] * n_pages), *([cache_vt] * n_pages))


def _out_kernel(x_ref, o_ref, ga_ref, sma_ref, gp_ref, gate_ref, wbra_ref, wout_ref, y_ref):
    h_a = (o_ref[...].astype(jnp.float32) * ga_ref[...].astype(jnp.float32)).astype(jnp.bfloat16)
    br_a = _dot(h_a, wbra_ref[...])
    merged = sma_ref[...].astype(jnp.float32) * br_a + gp_ref[...].astype(jnp.float32)
    y_ref[...] = x_ref[...] + gate_ref[...] * _dot(merged.astype(jnp.bfloat16), wout_ref[...])


def _out_proj(x, o, ga, sma, gp, gate, w_br_a, w_out, *, tm, tiles_per_seq):
    rows, d = x.shape

    def rowblk(width):
        return pl.BlockSpec((tm, width), lambda i: (i, 0))

    if gate.ndim == 3:
        gate_spec = pl.BlockSpec((None, 1, d), lambda i: (i // tiles_per_seq, 0, 0))
    else:
        gate_spec = pl.BlockSpec((tm, d), lambda i: (i, 0))
    return pl.pallas_call(
        _out_kernel,
        grid=(rows // tm,),
        in_specs=[rowblk(d), rowblk(D_ATTN), rowblk(D_ATTN), rowblk(d), rowblk(d), gate_spec,
                  pl.BlockSpec(w_br_a.shape, lambda i: (0, 0)), pl.BlockSpec(w_out.shape, lambda i: (0, 0))],
        out_specs=rowblk(d),
        out_shape=jax.ShapeDtypeStruct((rows, d), jnp.float32),
        compiler_params=pltpu.CompilerParams(dimension_semantics=("arbitrary",), vmem_limit_bytes=VMEM_LIMIT),
        name="out_proj",
    )(x, o, ga, sma, gp, gate, w_br_a, w_out)


def kernel(x_prompt, x_sample, cache_k, cache_v, cache_logf, state_pool, page_table, c_prompt, c_sample,
           norm_g, w_ada, b_ada, w_in, b_f, q_norm_g, k_norm_g, w_pool_grp, pool_scale, w_br_a, w_br_p, w_out):
    f32, bf = jnp.float32, jnp.bfloat16
    b, t, d = x_prompt.shape
    db, s_new, _ = x_sample.shape
    depth = w_in.shape[0]
    page = cache_k.shape[2]
    n_pages = page_table.shape[1]
    tm = 512
    tq = 512
    assert t % tm == 0 and t % tq == 0 and db % 8 == 0 and s_new <= Q_ROWS and n_pages == page_table.shape[1]

    o_q, o_k, o_v = 0, D_ATTN, 2 * D_ATTN
    o_f = 3 * D_ATTN
    o_ga = o_f + N_HEADS
    o_z = o_ga + D_ATTN
    o_gp = o_z + D_POOL
    o_m = o_gp + D_POOL
    w_f = jnp.pad(w_in[:, :, o_f:o_ga], ((0, 0), (0, 0), (0, F_PAD - N_HEADS)))
    w_in_p = jnp.concatenate([w_in[:, :, o_q:o_f], w_in[:, :, o_ga:], w_f], axis=2).astype(bf)
    b_f_p = jnp.pad(b_f, ((0, 0), (0, F_PAD - N_HEADS))).reshape(depth, 1, F_PAD)
    q_gain = jnp.tile(q_norm_g, (1, N_HEADS)).reshape(depth, 1, D_ATTN)
    k_gain = jnp.tile(k_norm_g, (1, N_HEADS)).reshape(depth, 1, D_ATTN)
    w_pool_bf = w_pool_grp.astype(bf)
    w_br_a_bf, w_br_p_bf, w_out_bf = w_br_a.astype(bf), w_br_p.astype(bf), w_out.astype(bf)

    ri = lax.broadcasted_iota(jnp.int32, (3 * LANES, N_HEADS * LANES), 0)
    ci = lax.broadcasted_iota(jnp.int32, (3 * LANES, N_HEADS * LANES), 1)
    piece, src_head = ri // LANES, ri % LANES
    off = ci % LANES - HEAD_DIM
    consts = {"place": ((src_head == ci // LANES) & ((off == piece) | (off == piece + 3))).astype(bf)}
    rt = lax.broadcasted_iota(jnp.int32, (tm, tm), 0)
    ct = lax.broadcasted_iota(jnp.int32, (tm, tm), 1)
    consts["tri"] = (ct <= rt).astype(bf)
    rp = lax.broadcasted_iota(jnp.int32, (page, page), 0)
    cp = lax.broadcasted_iota(jnp.int32, (page, page), 1)
    lower = (rp > cp).astype(bf)
    cache_kt = jnp.transpose(cache_k, (0, 1, 3, 4, 2))
    cache_vt = jnp.transpose(cache_v, (0, 1, 3, 4, 2))
    logf_t = jnp.transpose(cache_logf, (0, 1, 3, 2))

    mc = db + 8
    c_all = jnp.concatenate([c_sample, c_prompt, jnp.zeros((mc - db - b, d), f32)], axis=0)
    mod = _modulation(c_all, w_ada, b_ada)

    hp = x_prompt.reshape(b * t, d)
    hs = jnp.transpose(x_sample, (1, 0, 2)).reshape(s_new * db, d)
    hist_zero = jnp.zeros((POOL_SLOTS, D_POOL), f32)
    tiles_per_seq = t // tm

    outs = {n: [] for n in ("kp", "vp", "fp", "pp", "ks", "vs", "fs", "ps")}
    for l in range(depth):
        lw = {"norm_g": norm_g[l].reshape(1, d), "w_in": w_in_p[l], "b_f": b_f_p[l], "q_gain": q_gain[l],
              "k_gain": k_gain[l], "w_pool": w_pool_bf[l], "pool_scale": pool_scale[l].reshape(1, D_POOL),
              "w_br_p": w_br_p_bf[l]}
        mod_s = mod[l, :db]
        mod_p = mod[l, db:db + b]
        shift_p, scale_p, gate_p = (mod_p[:, j * d:(j + 1) * d].reshape(b, 1, d) for j in range(3))
        mod_s_rows = jnp.tile(mod_s, (s_new, 1))
        shift_s, scale_s, gate_s = (mod_s_rows[:, j * d:(j + 1) * d] for j in range(3))

        (q_p, k32_p, kb_p, v32_p, vb_p, lf_p, ga_p, sma_p, gp_p, tail_p) = _inproj(
            hp, scale_p, shift_p, hist_zero, lw, consts, tm=tm, tstride=1, tiles_per_seq=tiles_per_seq,
            prompt=True)
        w_aug = N_HEADS * LANES
        o_p = _prompt_attention(q_p.reshape(b, t, w_aug), kb_p.reshape(b, t, w_aug), vb_p.reshape(b, t, w_aug),
                                tq=tq)
        hp = _out_proj(hp, o_p.reshape(b * t, D_ATTN), ga_p, sma_p, gp_p, gate_p, w_br_a_bf[l], w_out_bf[l],
                       tm=tm, tiles_per_seq=tiles_per_seq)
        outs["kp"].append(k32_p.reshape(b, t, N_HEADS, HEAD_DIM))
        outs["vp"].append(v32_p.reshape(b, t, N_HEADS, HEAD_DIM))
        outs["fp"].append(lf_p.reshape(b, t, N_HEADS))
        outs["pp"].append(tail_p[:, 1:, :])

        hist_s = jnp.concatenate([jnp.zeros((1, db, D_POOL), f32), jnp.transpose(state_pool[l], (1, 0, 2))],
                                 axis=0).reshape(POOL_SLOTS * db, D_POOL)
        (q_s, k32_s, kb_s, v32_s, vb_s, lf_s, ga_s, sma_s, gp_s, tail_s) = _inproj(
            hs, scale_s, shift_s, hist_s, lw, consts, tm=s_new * db, tstride=db, tiles_per_seq=1,
            prompt=False)
        q16 = jnp.transpose(q_s.reshape(s_new, db, N_HEADS, HEAD_DIM), (1, 2, 0, 3))
        q16 = jnp.pad(q16, ((0, 0), (0, 0), (0, Q_ROWS - s_new), (0, 0)))
        kn_t = jnp.transpose(kb_s.reshape(s_new, db, N_HEADS, HEAD_DIM), (1, 2, 3, 0))
        kn_t = jnp.pad(kn_t, ((0, 0), (0, 0), (0, 0), (0, page - s_new)))
        vn_t = jnp.transpose(vb_s.reshape(s_new, db, N_HEADS, HEAD_DIM), (1, 2, 3, 0))
        vn_t = jnp.pad(vn_t, ((0, 0), (0, 0), (0, 0), (0, page - s_new)))
        lf_bt = jnp.transpose(lf_s.reshape(s_new, db, N_HEADS), (1, 0, 2))
        lf_bh = jnp.transpose(lf_bt, (0, 2, 1))
        lfa = jnp.broadcast_to(jnp.pad(lf_bh, ((0, 0), (0, 0), (0, page - s_new)))[:, :, None, :],
                               (db, N_HEADS, Q_ROWS, page))
        lfb = jnp.broadcast_to(jnp.pad(lf_bh, ((0, 0), (0, 0), (0, Q_ROWS - s_new)))[:, :, :, None],
                               (db, N_HEADS, Q_ROWS, page))
        o_s = _sample_attention(page_table, q16, kn_t, vn_t, lfa, lfb, lower, logf_t, cache_kt, cache_vt, l,
                                s_new=s_new)
        o_s = jnp.transpose(o_s[:, :, :s_new, :], (2, 0, 1, 3)).reshape(s_new * db, D_ATTN).astype(bf)
        hs = _out_proj(hs, o_s, ga_s, sma_s, gp_s, gate_s, w_br_a_bf[l], w_out_bf[l], tm=s_new * db,
                       tiles_per_seq=1)
        outs["ks"].append(jnp.transpose(k32_s.reshape(s_new, db, N_HEADS, HEAD_DIM), (1, 0, 2, 3)))
        outs["vs"].append(jnp.transpose(v32_s.reshape(s_new, db, N_HEADS, HEAD_DIM), (1, 0, 2, 3)))
        outs["fs"].append(lf_bt)
        outs["ps"].append(jnp.transpose(tail_s.reshape(POOL_SLOTS, db, D_POOL)[1:], (1, 0, 2)))

    y_p = hp.reshape(b, t, d)
    y_s = jnp.transpose(hs.reshape(s_new, db, d), (1, 0, 2))
    return (y_p, y_s, jnp.stack(outs["kp"]), jnp.stack(outs["vp"]), jnp.stack(outs["fp"]),
            jnp.stack(outs["pp"]), jnp.stack(outs["ks"]), jnp.stack(outs["vs"]), jnp.stack(outs["fs"]),
            jnp.stack(outs["ps"]))
```

```python
import functools
import math

import jax
import jax.numpy as jnp
from jax import lax
from jax.experimental import pallas as pl
from jax.experimental.pallas import tpu as pltpu

N_HEADS = 8
HEAD_DIM = 64
D_ATTN = N_HEADS * HEAD_DIM
POOL_WINDOWS = (2, 4, 8, 16)
POOL_GC = 128
D_POOL = POOL_GC * len(POOL_WINDOWS)
POOL_HIST = max(POOL_WINDOWS) - 1
POOL_SLOTS = POOL_HIST + 1
EPS = 1e-6
LOG2E = math.log2(math.e)
LANES = 128
F_PAD = LANES
Q_ROWS = 16
VMEM_LIMIT = 56 * 1024 * 1024

C_Q, C_K, C_V, C_GA, C_Z, C_GP = 0, 512, 1024, 1536, 2048, 2560
C_MA = 3072


def _split3(x):
    hi = x.astype(jnp.bfloat16)
    r1 = x - hi.astype(jnp.float32)
    mid = r1.astype(jnp.bfloat16)
    lo = (r1 - mid.astype(jnp.float32)).astype(jnp.bfloat16)
    return hi, mid, lo


def _dot(a, b):
    return jnp.dot(a, b, preferred_element_type=jnp.float32)


def _dot_exact_rhs(sel, x):
    hi, mid, lo = _split3(x)
    return _dot(sel, hi) + _dot(sel, mid) + _dot(sel, lo)


def _dot_exact_lhs(x, sel):
    hi, mid, lo = _split3(x)
    return _dot(hi, sel) + _dot(mid, sel) + _dot(lo, sel)


def _div_pow2(x, n):
    assert n & (n - 1) == 0
    return lax.shift_right_logical(x, n.bit_length() - 1)


def _mod_pow2(x, n):
    assert n & (n - 1) == 0
    return lax.bitwise_and(x, n - 1)


def _silu(x):
    return x * jax.nn.sigmoid(x)


def _log_sigmoid(x):
    return jnp.minimum(x, 0.0) - jnp.log1p(jnp.exp(-jnp.abs(x)))


def _mod_kernel(c_ref, w_ref, b_ref, o_ref):
    a_hi, a_mid, a_lo = _split3(_silu(c_ref[...]))
    w_hi, w_mid, w_lo = _split3(w_ref[...])
    acc = _dot(a_hi, w_hi) + (_dot(a_hi, w_mid) + _dot(a_mid, w_hi))
    acc = acc + (_dot(a_hi, w_lo) + _dot(a_lo, w_hi) + _dot(a_mid, w_mid))
    o_ref[...] = acc + b_ref[...]


def _modulation(c_all, w_ada_bf, b_ada):
    depth, d, d3 = w_ada_bf.shape
    mc = c_all.shape[0]
    tn = 512
    return pl.pallas_call(
        _mod_kernel,
        grid=(depth, d3 // tn),
        in_specs=[pl.BlockSpec((mc, d), lambda l, j: (0, 0)),
                  pl.BlockSpec((None, d, tn), lambda l, j: (l, 0, j)),
                  pl.BlockSpec((None, 1, tn), lambda l, j: (l, 0, j))],
        out_specs=pl.BlockSpec((None, mc, tn), lambda l, j: (l, 0, j)),
        out_shape=jax.ShapeDtypeStruct((depth, mc, d3), jnp.float32),
        name="adaln_mod",
    )(c_all, w_ada_bf, b_ada.reshape(depth, 1, d3))


def _inproj_kernel(x_ref, scale_ref, shift_ref, g_ref, w_ref, bf_ref, qg_ref, kg_ref, place_ref, tri_ref,
                   hist_ref, wpool_ref, pscale_ref, wbrp_ref,
                   q_ref, k32_ref, kb_ref, v32_ref, vb_ref, logf_ref, ga_ref, sma_ref, gp_ref, tail_ref,
                   zext_scr, fcarry_scr, *, tm, tstride, tiles_per_seq, prompt):
    hist_rows = POOL_SLOTS * tstride
    i = pl.program_id(0)
    tile_in_seq = lax.rem(i, tiles_per_seq)
    first = tile_in_seq == 0

    @pl.when(first)
    def _():
        zext_scr[0:hist_rows, :] = hist_ref[...]
        fcarry_scr[...] = jnp.zeros_like(fcarry_scr)

    x = x_ref[...]
    ms = jnp.mean(x * x, axis=-1, keepdims=True)
    xn = x * lax.rsqrt(ms + EPS) * g_ref[...]
    u = (xn * (1.0 + scale_ref[...]) + shift_ref[...]).astype(jnp.bfloat16)

    def proj(c0, width):
        return _dot(u, w_ref[:, c0:c0 + width])

    lane = lax.broadcasted_iota(jnp.int32, (tm, LANES), 1)
    low_half = lane < HEAD_DIM

    def head_norm(p, gain_ref):
        outs = []
        for j in range(D_ATTN // LANES):
            slab = p[:, j * LANES:(j + 1) * LANES]
            sq = slab * slab
            s_lo = jnp.sum(jnp.where(low_half, sq, 0.0), axis=1, keepdims=True)
            s_hi = jnp.sum(jnp.where(low_half, 0.0, sq), axis=1, keepdims=True)
            r = jnp.where(low_half, lax.rsqrt(s_lo * (1.0 / HEAD_DIM) + EPS), lax.rsqrt(s_hi * (1.0 / HEAD_DIM) + EPS))
            outs.append(slab * r)
        return jnp.concatenate(outs, axis=1) * gain_ref[...]

    c_f = w_ref.shape[1] - F_PAD
    logf = _log_sigmoid(proj(c_f, F_PAD) + bf_ref[...])
    logf_ref[...] = logf[:, :N_HEADS]

    qn = head_norm(proj(C_Q, D_ATTN), qg_ref) * (HEAD_DIM ** -0.5 * LOG2E)
    kn = head_norm(proj(C_K, D_ATTN), kg_ref)
    k32_ref[...] = kn
    pv = proj(C_V, D_ATTN)
    v32_ref[...] = pv
    if prompt:
        fc = _dot_exact_rhs(tri_ref[...], logf) + fcarry_scr[...]
        fcarry_scr[...] = fc[tm - 1:tm, :]
        hi, mid, lo = _split3(fc * LOG2E)
        extras = _dot(jnp.concatenate([hi, mid, lo], axis=1), place_ref[...])
        for h in range(N_HEADS):
            j = h // 2
            e_h = extras[:, h * LANES:(h + 1) * LANES]

            def head_tile(x, h=h, j=j):
                slab = x[:, j * LANES:(j + 1) * LANES]
                return slab if h % 2 == 0 else pltpu.roll(slab, HEAD_DIM, axis=1)

            q_t = jnp.where(low_half, head_tile(qn), jnp.where(lane < HEAD_DIM + 3, 1.0, e_h))
            k_t = jnp.where(low_half, head_tile(kn),
                            jnp.where(lane < HEAD_DIM + 3, -e_h, jnp.where(lane < HEAD_DIM + 6, 1.0, 0.0)))
            v_t = jnp.where(low_half, head_tile(pv), jnp.where(lane == HEAD_DIM, 1.0, 0.0))
            q_ref[:, h * LANES:(h + 1) * LANES] = q_t.astype(jnp.bfloat16)
            kb_ref[:, h * LANES:(h + 1) * LANES] = k_t.astype(jnp.bfloat16)
            vb_ref[:, h * LANES:(h + 1) * LANES] = v_t.astype(jnp.bfloat16)
    else:
        q_ref[...] = qn.astype(jnp.bfloat16)
        kb_ref[...] = kn.astype(jnp.bfloat16)
        vb_ref[...] = pv.astype(jnp.bfloat16)

    ga_ref[...] = _silu(proj(C_GA, D_ATTN)).astype(jnp.bfloat16)

    zext_scr[hist_rows:hist_rows + tm, :] = proj(C_Z, D_POOL)
    if prompt:
        pos = tile_in_seq * tm + lax.broadcasted_iota(jnp.int32, (tm, 1), 0)
    ys = []
    for g, w in enumerate(POOL_WINDOWS):
        col = zext_scr[:, g * POOL_GC:(g + 1) * POOL_GC]
        s = col
        span = 1
        while span < w:
            s = s + pltpu.roll(s, span * tstride, axis=0)
            span *= 2
        if prompt:
            cnt = jnp.minimum(pos + 1, w).astype(jnp.float32)
        else:
            cnt = jnp.float32(w)
        diff = s[hist_rows:, :] / cnt - col[hist_rows:, :]
        ys.append(_dot(diff.astype(jnp.bfloat16), wpool_ref[g]))
    y_pool = jnp.concatenate(ys, axis=1) * pscale_ref[...]
    tail = zext_scr[tm:tm + hist_rows, :]
    tail_ref[...] = tail
    if tiles_per_seq > 1:
        zext_scr[0:hist_rows, :] = tail

    h_p = (y_pool * _silu(proj(C_GP, D_POOL))).astype(jnp.bfloat16)
    d_model = x.shape[1]
    half = d_model // 2
    for c in range(2):
        br_p = _dot(h_p, wbrp_ref[:, c * half:(c + 1) * half])
        m_p = proj(C_MA + d_model + c * half, half)
        gp_ref[:, c * half:(c + 1) * half] = (jax.nn.sigmoid(m_p) * br_p).astype(jnp.bfloat16)
        m_a = proj(C_MA + c * half, half)
        sma_ref[:, c * half:(c + 1) * half] = jax.nn.sigmoid(m_a).astype(jnp.bfloat16)


def _inproj(x, scale, shift, hist, lw, consts, *, tm, tstride, tiles_per_seq, prompt):
    rows, d = x.shape
    n_tiles = rows // tm
    hist_rows = POOL_SLOTS * tstride
    w_in = lw["w_in"]
    bf, f32 = jnp.bfloat16, jnp.float32

    def full(a):
        return pl.BlockSpec(a.shape, lambda i, nd=a.ndim: (0,) * nd)

    if scale.ndim == 3:
        mod_spec = pl.BlockSpec((None, 1, d), lambda i: (i // tiles_per_seq, 0, 0))
    else:
        mod_spec = pl.BlockSpec((tm, d), lambda i: (i, 0))

    def rowblk(width):
        return pl.BlockSpec((tm, width), lambda i: (i, 0))

    n_seq = n_tiles // tiles_per_seq
    w_att = N_HEADS * LANES if prompt else D_ATTN
    out_shape = (
        jax.ShapeDtypeStruct((rows, w_att), bf),
        jax.ShapeDtypeStruct((rows, D_ATTN), f32),
        jax.ShapeDtypeStruct((rows, w_att), bf),
        jax.ShapeDtypeStruct((rows, D_ATTN), f32),
        jax.ShapeDtypeStruct((rows, w_att), bf),
        jax.ShapeDtypeStruct((rows, N_HEADS), f32),
        jax.ShapeDtypeStruct((rows, D_ATTN), bf),
        jax.ShapeDtypeStruct((rows, d), bf),
        jax.ShapeDtypeStruct((rows, d), bf),
        jax.ShapeDtypeStruct((n_seq, hist_rows, D_POOL), f32),
    )
    out_specs = (rowblk(w_att), rowblk(D_ATTN), rowblk(w_att), rowblk(D_ATTN), rowblk(w_att),
                 rowblk(N_HEADS), rowblk(D_ATTN), rowblk(d), rowblk(d),
                 pl.BlockSpec((None, hist_rows, D_POOL), lambda i: (i // tiles_per_seq, 0, 0)))
    kern = functools.partial(_inproj_kernel, tm=tm, tstride=tstride, tiles_per_seq=tiles_per_seq,
                             prompt=prompt)
    return pl.pallas_call(
        kern,
        grid=(n_tiles,),
        in_specs=[rowblk(d), mod_spec, mod_spec, full(lw["norm_g"]), full(w_in), full(lw["b_f"]),
                  full(lw["q_gain"]), full(lw["k_gain"]), full(consts["place"]), full(consts["tri"]),
                  full(hist), full(lw["w_pool"]), full(lw["pool_scale"]), full(lw["w_br_p"])],
        out_specs=out_specs,
        out_shape=out_shape,
        scratch_shapes=[pltpu.VMEM((hist_rows + tm, D_POOL), f32), pltpu.VMEM((1, F_PAD), f32)],
        compiler_params=pltpu.CompilerParams(dimension_semantics=("arbitrary",), vmem_limit_bytes=VMEM_LIMIT),
        name="in_proj",
    )(x, scale, shift, lw["norm_g"], w_in, lw["b_f"], lw["q_gain"], lw["k_gain"], consts["place"],
      consts["tri"], hist, lw["w_pool"], lw["pool_scale"], lw["w_br_p"])


def _attn_kernel(q_ref, k_ref, v_ref, o_ref, m_scr, acc_scr, s_scr, *, tq):
    qi = pl.program_id(2)
    row = lax.broadcasted_iota(jnp.int32, (tq, tq), 0)
    col = lax.broadcasted_iota(jnp.int32, (tq, tq), 1)
    reps = tq // LANES
    nt = (((1,), (1,)), ((), ()))
    m_scr[...] = jnp.full_like(m_scr, -jnp.inf)
    acc_scr[...] = jnp.zeros_like(acc_scr)

    def scores(kt, hh):
        start = pl.multiple_of(kt * tq, tq)
        sl = slice(hh * LANES, (hh + 1) * LANES)
        return lax.dot_general(q_ref[:, sl], k_ref[pl.ds(start, tq), sl], nt, preferred_element_type=jnp.float32)

    def accumulate(s, kt, hh, masked):
        start = pl.multiple_of(kt * tq, tq)
        sl = slice(hh * LANES, (hh + 1) * LANES)
        if masked:
            s = jnp.where(col <= row, s, -jnp.inf)
        m_prev = m_scr[hh]
        m_next = jnp.maximum(m_prev, jnp.max(s, axis=1, keepdims=True))
        p = jnp.exp2(s - jnp.tile(m_next, (1, reps)))
        alpha = jnp.exp2(m_prev - m_next)
        acc_scr[hh] = alpha * acc_scr[hh] + _dot(p.astype(jnp.bfloat16), v_ref[pl.ds(start, tq), sl])
        m_scr[hh] = m_next

    for hh in range(2):
        s_scr[hh] = scores(0, hh)

    def body(kt, carry):
        for hh in range(2):
            s = s_scr[hh]
            s_scr[hh] = scores(kt + 1, hh)
            accumulate(s, kt, hh, False)
        return carry

    lax.fori_loop(0, qi, body, 0)
    for hh in range(2):
        accumulate(s_scr[hh], qi, hh, True)
    lane = lax.broadcasted_iota(jnp.int32, (tq, LANES), 1)
    o0 = acc_scr[0] / acc_scr[0][:, HEAD_DIM:HEAD_DIM + 1]
    o1 = acc_scr[1] / acc_scr[1][:, HEAD_DIM:HEAD_DIM + 1]
    o_ref[...] = jnp.where(lane < HEAD_DIM, o0, pltpu.roll(o1, HEAD_DIM, axis=1)).astype(o_ref.dtype)


def _prompt_attention(q, k, v, *, tq):
    b, t, _ = q.shape
    n_t = t // tq
    pair = 2 * LANES
    kern = functools.partial(_attn_kernel, tq=tq)
    return pl.pallas_call(
        kern,
        grid=(b, N_HEADS // 2, n_t),
        in_specs=[pl.BlockSpec((None, tq, pair), lambda bi, hp, qi: (bi, qi, hp)),
                  pl.BlockSpec((None, t, pair), lambda bi, hp, qi: (bi, 0, hp)),
                  pl.BlockSpec((None, t, pair), lambda bi, hp, qi: (bi, 0, hp))],
        out_specs=pl.BlockSpec((None, tq, LANES), lambda bi, hp, qi: (bi, qi, hp)),
        out_shape=jax.ShapeDtypeStruct((b, t, D_ATTN), jnp.bfloat16),
        scratch_shapes=[pltpu.VMEM((2, tq, LANES), jnp.float32), pltpu.VMEM((2, tq, LANES), jnp.float32),
                        pltpu.VMEM((2, tq, tq), jnp.float32)],
        compiler_params=pltpu.CompilerParams(dimension_semantics=("arbitrary", "arbitrary", "arbitrary"),
                                             vmem_limit_bytes=VMEM_LIMIT),
        name="fox_prompt_attn",
    )(q, k, v)


def _sample_attn_kernel(pt_ref, q_ref, kn_ref, vn_ref, lfn_ref, lower_ref, lfc_hbm, kc_hbm, vc_hbm,
                        o_ref, kbuf, vbuf, lbuf, g_scr, sem, *, layer, n_pages, s_new):
    f32, bf = jnp.float32, jnp.bfloat16
    nt = (((1,), (1,)), ((), ()))
    page = kn_ref.shape[-1]
    b = pl.program_id(0)
    n_b = pl.num_programs(0)
    slot = lax.rem(b, 2)

    def page_copies(seq, slot_):
        out = []
        for p in range(n_pages):
            pg = pt_ref[seq, p]
            lanes = pl.ds(p * page, page)
            out.append(pltpu.make_async_copy(lfc_hbm.at[layer, pg], lbuf.at[slot_, :, lanes], sem.at[0, slot_]))
            out.append(pltpu.make_async_copy(kc_hbm.at[layer, pg], kbuf.at[slot_, :, :, lanes], sem.at[1, slot_]))
            out.append(pltpu.make_async_copy(vc_hbm.at[layer, pg], vbuf.at[slot_, :, :, lanes], sem.at[2, slot_]))
        return out

    @pl.when(b == 0)
    def _():
        for c in page_copies(0, 0):
            c.start()

    @pl.when(b + 1 < n_b)
    def _():
        for c in page_copies(b + 1, 1 - slot):
            c.start()

    for c in page_copies(b, slot):
        c.wait()

    carry = jnp.zeros((N_HEADS, 1), f32)
    for p in reversed(range(n_pages)):
        lf = lbuf[slot, :, p * page:(p + 1) * page]
        g_scr[:, p * page:(p + 1) * page] = (_dot_exact_lhs(lf, lower_ref[...]) + carry) * LOG2E
        carry = carry + jnp.sum(lf, axis=1, keepdims=True)

    qi = lax.broadcasted_iota(jnp.int32, (Q_ROWS, page), 0)
    ti = lax.broadcasted_iota(jnp.int32, (Q_ROWS, page), 1)
    new_ok = (ti <= qi) & (ti < s_new)
    q_row = lax.broadcasted_iota(jnp.int32, (Q_ROWS, 1), 0)
    t_lane = lax.broadcasted_iota(jnp.int32, (1, page), 1)
    for h in range(N_HEADS):
        qh = q_ref[h]
        lf_new = lfn_ref[h:h + 1, :]
        c_col = jnp.zeros((Q_ROWS, 1), f32)
        c_row = jnp.zeros((1, page), f32)
        for t_i in range(s_new):
            lf_t = lf_new[:, t_i:t_i + 1]
            c_col = c_col + jnp.where(q_row >= t_i, lf_t, 0.0)
            c_row = c_row + jnp.where(t_lane >= t_i, lf_t, 0.0)
        c_col = c_col * LOG2E
        c_row = c_row * LOG2E
        s_n = jnp.where(new_ok, _dot(qh, kn_ref[h]) + c_col - c_row, -jnp.inf)
        kt = kbuf[slot, h].astype(bf)
        vt = vbuf[slot, h].astype(bf)
        s = _dot(qh, kt) + c_col + g_scr[h:h + 1, :]
        m = jnp.maximum(jnp.max(s, axis=1, keepdims=True), jnp.max(s_n, axis=1, keepdims=True))
        pr = jnp.exp2(s - m)
        pr_n = jnp.exp2(s_n - m)
        l = jnp.sum(pr, axis=1, keepdims=True) + jnp.sum(pr_n, axis=1, keepdims=True)
        acc = (lax.dot_general(pr.astype(bf), vt, nt, preferred_element_type=f32)
               + lax.dot_general(pr_n.astype(bf), vn_ref[h], nt, preferred_element_type=f32))
        o_ref[h] = acc / l


def _sample_attention(page_table, q16, kn_t, vn_t, lf_new, lower, logf_t, cache_kt, cache_vt, layer, *, s_new):
    db, n_pages = page_table.shape
    heads, hd, page = cache_kt.shape[2:]
    past = n_pages * page
    f32 = jnp.float32

    def per_b(a):
        return pl.BlockSpec((None,) + a.shape[1:], lambda b, pt, nd=a.ndim: (b,) + (0,) * (nd - 1))

    hbm = pl.BlockSpec(memory_space=pl.ANY)
    grid_spec = pltpu.PrefetchScalarGridSpec(
        num_scalar_prefetch=1,
        grid=(db,),
        in_specs=[per_b(q16), per_b(kn_t), per_b(vn_t), per_b(lf_new),
                  pl.BlockSpec(lower.shape, lambda b, pt: (0, 0)), hbm, hbm, hbm],
        out_specs=pl.BlockSpec((None, heads, Q_ROWS, hd), lambda b, pt: (b, 0, 0, 0)),
        scratch_shapes=[pltpu.VMEM((2, heads, hd, past), f32), pltpu.VMEM((2, heads, hd, past), f32),
                        pltpu.VMEM((2, heads, past), f32), pltpu.VMEM((heads, past), f32),
                        pltpu.SemaphoreType.DMA((3, 2))],
    )
    kern = functools.partial(_sample_attn_kernel, layer=layer, n_pages=n_pages, s_new=s_new)
    return pl.pallas_call(
        kern,
        grid_spec=grid_spec,
        out_shape=jax.ShapeDtypeStruct((db, heads, Q_ROWS, hd), f32),
        compiler_params=pltpu.CompilerParams(dimension_semantics=("arbitrary",), vmem_limit_bytes=VMEM_LIMIT),
        name="fox_sample_attn",
    )(page_table, q16, kn_t, vn_t, lf_new, lower, logf_t, cache_kt, cache_vt)


def _out_kernel(x_ref, o_ref, ga_ref, sma_ref, gp_ref, gate_ref, wbra_ref, wout_ref, y_ref):
    h_a = (o_ref[...].astype(jnp.float32) * ga_ref[...].astype(jnp.float32)).astype(jnp.bfloat16)
    br_a = _dot(h_a, wbra_ref[...])
    merged = sma_ref[...].astype(jnp.float32) * br_a + gp_ref[...].astype(jnp.float32)
    y_ref[...] = x_ref[...] + gate_ref[...] * _dot(merged.astype(jnp.bfloat16), wout_ref[...])


def _out_proj(x, o, ga, sma, gp, gate, w_br_a, w_out, *, tm, tiles_per_seq):
    rows, d = x.shape

    def rowblk(width):
        return pl.BlockSpec((tm, width), lambda i: (i, 0))

    if gate.ndim == 3:
        gate_spec = pl.BlockSpec((None, 1, d), lambda i: (i // tiles_per_seq, 0, 0))
    else:
        gate_spec = pl.BlockSpec((tm, d), lambda i: (i, 0))
    return pl.pallas_call(
        _out_kernel,
        grid=(rows // tm,),
        in_specs=[rowblk(d), rowblk(D_ATTN), rowblk(D_ATTN), rowblk(d), rowblk(d), gate_spec,
                  pl.BlockSpec(w_br_a.shape, lambda i: (0, 0)), pl.BlockSpec(w_out.shape, lambda i: (0, 0))],
        out_specs=rowblk(d),
        out_shape=jax.ShapeDtypeStruct((rows, d), jnp.float32),
        compiler_params=pltpu.CompilerParams(dimension_semantics=("arbitrary",), vmem_limit_bytes=VMEM_LIMIT),
        name="out_proj",
    )(x, o, ga, sma, gp, gate, w_br_a, w_out)


def kernel(x_prompt, x_sample, cache_k, cache_v, cache_logf, state_pool, page_table, c_prompt, c_sample,
           norm_g, w_ada, b_ada, w_in, b_f, q_norm_g, k_norm_g, w_pool_grp, pool_scale, w_br_a, w_br_p, w_out):
    f32, bf = jnp.float32, jnp.bfloat16
    b, t, d = x_prompt.shape
    db, s_new, _ = x_sample.shape
    depth = w_in.shape[0]
    page = cache_k.shape[2]
    n_pages = page_table.shape[1]
    tm = 512
    tq = 512
    assert t % tm == 0 and t % tq == 0 and db % 8 == 0 and s_new <= Q_ROWS and n_pages == page_table.shape[1]

    o_q, o_k, o_v = 0, D_ATTN, 2 * D_ATTN
    o_f = 3 * D_ATTN
    o_ga = o_f + N_HEADS
    o_z = o_ga + D_ATTN
    o_gp = o_z + D_POOL
    o_m = o_gp + D_POOL
    w_f = jnp.pad(w_in[:, :, o_f:o_ga], ((0, 0), (0, 0), (0, F_PAD - N_HEADS)))
    w_in_p = jnp.concatenate([w_in[:, :, o_q:o_f], w_in[:, :, o_ga:], w_f], axis=2).astype(bf)
    b_f_p = jnp.pad(b_f, ((0, 0), (0, F_PAD - N_HEADS))).reshape(depth, 1, F_PAD)
    q_gain = jnp.tile(q_norm_g, (1, N_HEADS)).reshape(depth, 1, D_ATTN)
    k_gain = jnp.tile(k_norm_g, (1, N_HEADS)).reshape(depth, 1, D_ATTN)
    w_pool_bf = w_pool_grp.astype(bf)
    w_br_a_bf, w_br_p_bf, w_out_bf = w_br_a.astype(bf), w_br_p.astype(bf), w_out.astype(bf)

    ri = lax.broadcasted_iota(jnp.int32, (3 * LANES, N_HEADS * LANES), 0)
    ci = lax.broadcasted_iota(jnp.int32, (3 * LANES, N_HEADS * LANES), 1)
    piece, src_head = ri // LANES, ri % LANES
    off = ci % LANES - HEAD_DIM
    consts = {"place": ((src_head == ci // LANES) & ((off == piece) | (off == piece + 3))).astype(bf)}
    rt = lax.broadcasted_iota(jnp.int32, (tm, tm), 0)
    ct = lax.broadcasted_iota(jnp.int32, (tm, tm), 1)
    consts["tri"] = (ct <= rt).astype(bf)
    rp = lax.broadcasted_iota(jnp.int32, (page, page), 0)
    cp = lax.broadcasted_iota(jnp.int32, (page, page), 1)
    lower = (rp > cp).astype(bf)
    cache_kt = jnp.transpose(cache_k, (0, 1, 3, 4, 2))
    cache_vt = jnp.transpose(cache_v, (0, 1, 3, 4, 2))
    logf_t = jnp.transpose(cache_logf, (0, 1, 3, 2))

    mc = db + 8
    c_all = jnp.concatenate([c_sample, c_prompt, jnp.zeros((mc - db - b, d), f32)], axis=0)
    mod = _modulation(c_all, w_ada, b_ada)

    hp = x_prompt.reshape(b * t, d)
    hs = jnp.transpose(x_sample, (1, 0, 2)).reshape(s_new * db, d)
    hist_zero = jnp.zeros((POOL_SLOTS, D_POOL), f32)
    tiles_per_seq = t // tm

    outs = {n: [] for n in ("kp", "vp", "fp", "pp", "ks", "vs", "fs", "ps")}
    for l in range(depth):
        lw = {"norm_g": norm_g[l].reshape(1, d), "w_in": w_in_p[l], "b_f": b_f_p[l], "q_gain": q_gain[l],
              "k_gain": k_gain[l], "w_pool": w_pool_bf[l], "pool_scale": pool_scale[l].reshape(1, D_POOL),
              "w_br_p": w_br_p_bf[l]}
        mod_s = mod[l, :db]
        mod_p = mod[l, db:db + b]
        shift_p, scale_p, gate_p = (mod_p[:, j * d:(j + 1) * d].reshape(b, 1, d) for j in range(3))
        mod_s_rows = jnp.tile(mod_s, (s_new, 1))
        shift_s, scale_s, gate_s = (mod_s_rows[:, j * d:(j + 1) * d] for j in range(3))

        (q_p, k32_p, kb_p, v32_p, vb_p, lf_p, ga_p, sma_p, gp_p, tail_p) = _inproj(
            hp, scale_p, shift_p, hist_zero, lw, consts, tm=tm, tstride=1, tiles_per_seq=tiles_per_seq,
            prompt=True)
        w_aug = N_HEADS * LANES
        o_p = _prompt_attention(q_p.reshape(b, t, w_aug), kb_p.reshape(b, t, w_aug), vb_p.reshape(b, t, w_aug),
                                tq=tq)
        hp = _out_proj(hp, o_p.reshape(b * t, D_ATTN), ga_p, sma_p, gp_p, gate_p, w_br_a_bf[l], w_out_bf[l],
                       tm=tm, tiles_per_seq=tiles_per_seq)
        outs["kp"].append(k32_p.reshape(b, t, N_HEADS, HEAD_DIM))
        outs["vp"].append(v32_p.reshape(b, t, N_HEADS, HEAD_DIM))
        outs["fp"].append(lf_p.reshape(b, t, N_HEADS))
        outs["pp"].append(tail_p[:, 1:, :])

        hist_s = jnp.concatenate([jnp.zeros((1, db, D_POOL), f32), jnp.transpose(state_pool[l], (1, 0, 2))],
                                 axis=0).reshape(POOL_SLOTS * db, D_POOL)
        (q_s, k32_s, kb_s, v32_s, vb_s, lf_s, ga_s, sma_s, gp_s, tail_s) = _inproj(
            hs, scale_s, shift_s, hist_s, lw, consts, tm=s_new * db, tstride=db, tiles_per_seq=1,
            prompt=False)
        q16 = jnp.transpose(q_s.reshape(s_new, db, N_HEADS, HEAD_DIM), (1, 2, 0, 3))
        q16 = jnp.pad(q16, ((0, 0), (0, 0), (0, Q_ROWS - s_new), (0, 0)))
        kn_t = jnp.transpose(kb_s.reshape(s_new, db, N_HEADS, HEAD_DIM), (1, 2, 3, 0))
        kn_t = jnp.pad(kn_t, ((0, 0), (0, 0), (0, 0), (0, page - s_new)))
        vn_t = jnp.transpose(vb_s.reshape(s_new, db, N_HEADS, HEAD_DIM), (1, 2, 3, 0))
        vn_t = jnp.pad(vn_t, ((0, 0), (0, 0), (0, 0), (0, page - s_new)))
        lf_bt = jnp.transpose(lf_s.reshape(s_new, db, N_HEADS), (1, 0, 2))
        lf_bh = jnp.transpose(lf_bt, (0, 2, 1))
        lf_new = jnp.pad(lf_bh, ((0, 0), (0, 0), (0, page - s_new)))
        o_s = _sample_attention(page_table, q16, kn_t, vn_t, lf_new, lower, logf_t, cache_kt, cache_vt, l,
                                s_new=s_new)
        o_s = jnp.transpose(o_s[:, :, :s_new, :], (2, 0, 1, 3)).reshape(s_new * db, D_ATTN).astype(bf)
        hs = _out_proj(hs, o_s, ga_s, sma_s, gp_s, gate_s, w_br_a_bf[l], w_out_bf[l], tm=s_new * db,
                       tiles_per_seq=1)
        outs["ks"].append(jnp.transpose(k32_s.reshape(s_new, db, N_HEADS, HEAD_DIM), (1, 0, 2, 3)))
        outs["vs"].append(jnp.transpose(v32_s.reshape(s_new, db, N_HEADS, HEAD_DIM), (1, 0, 2, 3)))
        outs["fs"].append(lf_bt)
        outs["ps"].append(jnp.transpose(tail_s.reshape(POOL_SLOTS, db, D_POOL)[1:], (1, 0, 2)))

    y_p = hp.reshape(b, t, d)
    y_s = jnp.transpose(hs.reshape(s_new, db, d), (1, 0, 2))
    return (y_p, y_s, jnp.stack(outs["kp"]), jnp.stack(outs["vp"]), jnp.stack(outs["fp"]),
            jnp.stack(outs["pp"]), jnp.stack(outs["ks"]), jnp.stack(outs["vs"]), jnp.stack(outs["fs"]),
            jnp.stack(outs["ps"]))
```

```python
import functools
import math

import jax
import jax.numpy as jnp
from jax import lax
from jax.experimental import pallas as pl
from jax.experimental.pallas import tpu as pltpu

N_HEADS = 8
HEAD_DIM = 64
D_ATTN = N_HEADS * HEAD_DIM
POOL_WINDOWS = (2, 4, 8, 16)
POOL_GC = 128
D_POOL = POOL_GC * len(POOL_WINDOWS)
POOL_HIST = max(POOL_WINDOWS) - 1
POOL_SLOTS = POOL_HIST + 1
EPS = 1e-6
LOG2E = math.log2(math.e)
LANES = 128
F_PAD = LANES
Q_ROWS = 16
VMEM_LIMIT = 56 * 1024 * 1024

C_Q, C_K, C_V, C_GA, C_Z, C_GP = 0, 512, 1024, 1536, 2048, 2560
C_MA = 3072


def _split3(x):
    hi = x.astype(jnp.bfloat16)
    r1 = x - hi.astype(jnp.float32)
    mid = r1.astype(jnp.bfloat16)
    lo = (r1 - mid.astype(jnp.float32)).astype(jnp.bfloat16)
    return hi, mid, lo


def _dot(a, b):
    return jnp.dot(a, b, preferred_element_type=jnp.float32)


def _dot_exact_rhs(sel, x):
    hi, mid, lo = _split3(x)
    return _dot(sel, hi) + _dot(sel, mid) + _dot(sel, lo)


def _dot_exact_lhs(x, sel):
    hi, mid, lo = _split3(x)
    return _dot(hi, sel) + _dot(mid, sel) + _dot(lo, sel)


def _div_pow2(x, n):
    assert n & (n - 1) == 0
    return lax.shift_right_logical(x, n.bit_length() - 1)


def _mod_pow2(x, n):
    assert n & (n - 1) == 0
    return lax.bitwise_and(x, n - 1)


def _silu(x):
    return x * jax.nn.sigmoid(x)


def _log_sigmoid(x):
    return jnp.minimum(x, 0.0) - jnp.log1p(jnp.exp(-jnp.abs(x)))


def _mod_kernel(c_ref, w_ref, b_ref, o_ref):
    a_hi, a_mid, a_lo = _split3(_silu(c_ref[...]))
    w_hi, w_mid, w_lo = _split3(w_ref[...])
    acc = _dot(a_hi, w_hi) + (_dot(a_hi, w_mid) + _dot(a_mid, w_hi))
    acc = acc + (_dot(a_hi, w_lo) + _dot(a_lo, w_hi) + _dot(a_mid, w_mid))
    o_ref[...] = acc + b_ref[...]


def _modulation(c_all, w_ada_bf, b_ada):
    depth, d, d3 = w_ada_bf.shape
    mc = c_all.shape[0]
    tn = 512
    return pl.pallas_call(
        _mod_kernel,
        grid=(depth, d3 // tn),
        in_specs=[pl.BlockSpec((mc, d), lambda l, j: (0, 0)),
                  pl.BlockSpec((None, d, tn), lambda l, j: (l, 0, j)),
                  pl.BlockSpec((None, 1, tn), lambda l, j: (l, 0, j))],
        out_specs=pl.BlockSpec((None, mc, tn), lambda l, j: (l, 0, j)),
        out_shape=jax.ShapeDtypeStruct((depth, mc, d3), jnp.float32),
        name="adaln_mod",
    )(c_all, w_ada_bf, b_ada.reshape(depth, 1, d3))


N_INPROJ_IN = 14


def _inproj_kernel(*refs, tm, tstride, tiles_per_seq, prompt, n_alias):
    (x_ref, scale_ref, shift_ref, g_ref, w_ref, bf_ref, qg_ref, kg_ref, place_ref, tri_ref,
     hist_ref, wpool_ref, pscale_ref, wbrp_ref) = refs[:N_INPROJ_IN]
    (q_ref, k32_ref, kb_ref, v32_ref, vb_ref, logf_ref, ga_ref, sma_ref, gp_ref, tail_ref,
     zext_scr, fcarry_scr) = refs[N_INPROJ_IN + n_alias:]
    hist_rows = POOL_SLOTS * tstride
    i = pl.program_id(0)
    tile_in_seq = lax.rem(i, tiles_per_seq)
    first = tile_in_seq == 0

    @pl.when(first)
    def _():
        zext_scr[0:hist_rows, :] = hist_ref[...]
        fcarry_scr[...] = jnp.zeros_like(fcarry_scr)

    x = x_ref[...]
    ms = jnp.mean(x * x, axis=-1, keepdims=True)
    xn = x * lax.rsqrt(ms + EPS) * g_ref[...]
    u = (xn * (1.0 + scale_ref[...]) + shift_ref[...]).astype(jnp.bfloat16)

    def proj(c0, width):
        return _dot(u, w_ref[:, c0:c0 + width])

    lane = lax.broadcasted_iota(jnp.int32, (tm, LANES), 1)
    low_half = lane < HEAD_DIM

    def head_norm(p, gain_ref):
        outs = []
        for j in range(D_ATTN // LANES):
            slab = p[:, j * LANES:(j + 1) * LANES]
            sq = slab * slab
            s_lo = jnp.sum(jnp.where(low_half, sq, 0.0), axis=1, keepdims=True)
            s_hi = jnp.sum(jnp.where(low_half, 0.0, sq), axis=1, keepdims=True)
            r = jnp.where(low_half, lax.rsqrt(s_lo * (1.0 / HEAD_DIM) + EPS), lax.rsqrt(s_hi * (1.0 / HEAD_DIM) + EPS))
            outs.append(slab * r)
        return jnp.concatenate(outs, axis=1) * gain_ref[...]

    c_f = w_ref.shape[1] - F_PAD
    logf = _log_sigmoid(proj(c_f, F_PAD) + bf_ref[...])
    logf_ref[...] = logf[:, :N_HEADS]

    qn = head_norm(proj(C_Q, D_ATTN), qg_ref) * (HEAD_DIM ** -0.5 * LOG2E)
    kn = head_norm(proj(C_K, D_ATTN), kg_ref)
    pv = proj(C_V, D_ATTN)
    k32_ref[...] = kn.T if prompt else kn
    v32_ref[...] = pv.T if prompt else pv
    if prompt:
        fc = _dot_exact_rhs(tri_ref[...], logf) + fcarry_scr[...]
        fcarry_scr[...] = fc[tm - 1:tm, :]
        hi, mid, lo = _split3(fc * LOG2E)
        extras = _dot(jnp.concatenate([hi, mid, lo], axis=1), place_ref[...])
        for h in range(N_HEADS):
            j = h // 2
            e_h = extras[:, h * LANES:(h + 1) * LANES]

            def head_tile(x, h=h, j=j):
                slab = x[:, j * LANES:(j + 1) * LANES]
                return slab if h % 2 == 0 else pltpu.roll(slab, HEAD_DIM, axis=1)

            q_t = jnp.where(low_half, head_tile(qn), jnp.where(lane < HEAD_DIM + 3, 1.0, e_h))
            k_t = jnp.where(low_half, head_tile(kn),
                            jnp.where(lane < HEAD_DIM + 3, -e_h, jnp.where(lane < HEAD_DIM + 6, 1.0, 0.0)))
            v_t = jnp.where(low_half, head_tile(pv), jnp.where(lane == HEAD_DIM, 1.0, 0.0))
            q_ref[:, h * LANES:(h + 1) * LANES] = q_t.astype(jnp.bfloat16)
            kb_ref[:, h * LANES:(h + 1) * LANES] = k_t.astype(jnp.bfloat16)
            vb_ref[:, h * LANES:(h + 1) * LANES] = v_t.astype(jnp.bfloat16)
    else:
        q_ref[...] = qn.astype(jnp.bfloat16)
        kb_ref[...] = kn.astype(jnp.bfloat16)
        vb_ref[...] = pv.astype(jnp.bfloat16)

    ga_ref[...] = _silu(proj(C_GA, D_ATTN)).astype(jnp.bfloat16)

    zext_scr[hist_rows:hist_rows + tm, :] = proj(C_Z, D_POOL)
    if prompt:
        pos = tile_in_seq * tm + lax.broadcasted_iota(jnp.int32, (tm, 1), 0)
    ys = []
    for g, w in enumerate(POOL_WINDOWS):
        col = zext_scr[:, g * POOL_GC:(g + 1) * POOL_GC]
        s = col
        span = 1
        while span < w:
            s = s + pltpu.roll(s, span * tstride, axis=0)
            span *= 2
        if prompt:
            cnt = jnp.minimum(pos + 1, w).astype(jnp.float32)
        else:
            cnt = jnp.float32(w)
        diff = s[hist_rows:, :] / cnt - col[hist_rows:, :]
        ys.append(_dot(diff.astype(jnp.bfloat16), wpool_ref[g]))
    y_pool = jnp.concatenate(ys, axis=1) * pscale_ref[...]
    tail = zext_scr[tm:tm + hist_rows, :]
    tail_ref[...] = tail
    if tiles_per_seq > 1:
        zext_scr[0:hist_rows, :] = tail

    h_p = (y_pool * _silu(proj(C_GP, D_POOL))).astype(jnp.bfloat16)
    d_model = x.shape[1]
    half = d_model // 2
    for c in range(2):
        br_p = _dot(h_p, wbrp_ref[:, c * half:(c + 1) * half])
        m_p = proj(C_MA + d_model + c * half, half)
        gp_ref[:, c * half:(c + 1) * half] = (jax.nn.sigmoid(m_p) * br_p).astype(jnp.bfloat16)
        m_a = proj(C_MA + c * half, half)
        sma_ref[:, c * half:(c + 1) * half] = jax.nn.sigmoid(m_a).astype(jnp.bfloat16)


def _inproj(x, scale, shift, hist, lw, consts, *, tm, tstride, tiles_per_seq, prompt, stacked=None, kv_bufs=()):
    rows, d = x.shape
    n_tiles = rows // tm
    hist_rows = POOL_SLOTS * tstride
    w_in = lw["w_in"]
    bf, f32 = jnp.bfloat16, jnp.float32

    def full(a):
        return pl.BlockSpec(a.shape, lambda i, nd=a.ndim: (0,) * nd)

    if scale.ndim == 3:
        mod_spec = pl.BlockSpec((None, 1, d), lambda i: (i // tiles_per_seq, 0, 0))
    else:
        mod_spec = pl.BlockSpec((tm, d), lambda i: (i, 0))

    def rowblk(width):
        return pl.BlockSpec((tm, width), lambda i: (i, 0))

    n_seq = n_tiles // tiles_per_seq
    w_att = N_HEADS * LANES if prompt else D_ATTN
    if prompt:
        layer, depth = stacked
        kv_shape = jax.ShapeDtypeStruct((depth, n_seq, D_ATTN, tiles_per_seq * tm), f32)
        kv_spec = pl.BlockSpec((None, None, D_ATTN, tm),
                               lambda i: (layer, i // tiles_per_seq, 0, lax.rem(i, tiles_per_seq)))
    else:
        kv_shape = jax.ShapeDtypeStruct((rows, D_ATTN), f32)
        kv_spec = rowblk(D_ATTN)
    n_alias = len(kv_bufs)
    out_shape = (
        jax.ShapeDtypeStruct((rows, w_att), bf),
        kv_shape,
        jax.ShapeDtypeStruct((rows, w_att), bf),
        kv_shape,
        jax.ShapeDtypeStruct((rows, w_att), bf),
        jax.ShapeDtypeStruct((rows, N_HEADS), f32),
        jax.ShapeDtypeStruct((rows, D_ATTN), bf),
        jax.ShapeDtypeStruct((rows, d), bf),
        jax.ShapeDtypeStruct((rows, d), bf),
        jax.ShapeDtypeStruct((n_seq, hist_rows, D_POOL), f32),
    )
    out_specs = (rowblk(w_att), kv_spec, rowblk(w_att), kv_spec, rowblk(w_att),
                 rowblk(N_HEADS), rowblk(D_ATTN), rowblk(d), rowblk(d),
                 pl.BlockSpec((None, hist_rows, D_POOL), lambda i: (i // tiles_per_seq, 0, 0)))
    kern = functools.partial(_inproj_kernel, tm=tm, tstride=tstride, tiles_per_seq=tiles_per_seq,
                             prompt=prompt, n_alias=n_alias)
    operands = (x, scale, shift, lw["norm_g"], w_in, lw["b_f"], lw["q_gain"], lw["k_gain"], consts["place"],
                consts["tri"], hist, lw["w_pool"], lw["pool_scale"], lw["w_br_p"])
    assert len(operands) == N_INPROJ_IN
    return pl.pallas_call(
        kern,
        grid=(n_tiles,),
        in_specs=[rowblk(d), mod_spec, mod_spec, full(lw["norm_g"]), full(w_in), full(lw["b_f"]),
                  full(lw["q_gain"]), full(lw["k_gain"]), full(consts["place"]), full(consts["tri"]),
                  full(hist), full(lw["w_pool"]), full(lw["pool_scale"]), full(lw["w_br_p"])]
                 + [pl.BlockSpec(memory_space=pl.ANY)] * n_alias,
        out_specs=out_specs,
        out_shape=out_shape,
        input_output_aliases={N_INPROJ_IN: 1, N_INPROJ_IN + 1: 3} if n_alias else {},
        scratch_shapes=[pltpu.VMEM((hist_rows + tm, D_POOL), f32), pltpu.VMEM((1, F_PAD), f32)],
        compiler_params=pltpu.CompilerParams(dimension_semantics=("arbitrary",), vmem_limit_bytes=VMEM_LIMIT),
        name="in_proj",
    )(*operands, *kv_bufs)


def _attn_kernel(q_ref, k_ref, v_ref, o_ref, m_scr, acc_scr, s_scr, *, tq):
    qi = pl.program_id(2)
    row = lax.broadcasted_iota(jnp.int32, (tq, tq), 0)
    col = lax.broadcasted_iota(jnp.int32, (tq, tq), 1)
    reps = tq // LANES
    nt = (((1,), (1,)), ((), ()))
    m_scr[...] = jnp.full_like(m_scr, -jnp.inf)
    acc_scr[...] = jnp.zeros_like(acc_scr)

    def scores(kt, hh):
        start = pl.multiple_of(kt * tq, tq)
        sl = slice(hh * LANES, (hh + 1) * LANES)
        return lax.dot_general(q_ref[:, sl], k_ref[pl.ds(start, tq), sl], nt, preferred_element_type=jnp.float32)

    def accumulate(s, kt, hh, masked):
        start = pl.multiple_of(kt * tq, tq)
        sl = slice(hh * LANES, (hh + 1) * LANES)
        if masked:
            s = jnp.where(col <= row, s, -jnp.inf)
        m_prev = m_scr[hh]
        m_next = jnp.maximum(m_prev, jnp.max(s, axis=1, keepdims=True))
        p = jnp.exp2(s - jnp.tile(m_next, (1, reps)))
        alpha = jnp.exp2(m_prev - m_next)
        acc_scr[hh] = alpha * acc_scr[hh] + _dot(p.astype(jnp.bfloat16), v_ref[pl.ds(start, tq), sl])
        m_scr[hh] = m_next

    for hh in range(2):
        s_scr[hh] = scores(0, hh)

    def body(kt, carry):
        for hh in range(2):
            s = s_scr[hh]
            s_scr[hh] = scores(kt + 1, hh)
            accumulate(s, kt, hh, False)
        return carry

    lax.fori_loop(0, qi, body, 0)
    for hh in range(2):
        accumulate(s_scr[hh], qi, hh, True)
    lane = lax.broadcasted_iota(jnp.int32, (tq, LANES), 1)
    o0 = acc_scr[0] / acc_scr[0][:, HEAD_DIM:HEAD_DIM + 1]
    o1 = acc_scr[1] / acc_scr[1][:, HEAD_DIM:HEAD_DIM + 1]
    o_ref[...] = jnp.where(lane < HEAD_DIM, o0, pltpu.roll(o1, HEAD_DIM, axis=1)).astype(o_ref.dtype)


def _prompt_attention(q, k, v, *, tq):
    b, t, _ = q.shape
    n_t = t // tq
    pair = 2 * LANES
    kern = functools.partial(_attn_kernel, tq=tq)
    return pl.pallas_call(
        kern,
        grid=(b, N_HEADS // 2, n_t),
        in_specs=[pl.BlockSpec((None, tq, pair), lambda bi, hp, qi: (bi, qi, hp)),
                  pl.BlockSpec((None, t, pair), lambda bi, hp, qi: (bi, 0, hp)),
                  pl.BlockSpec((None, t, pair), lambda bi, hp, qi: (bi, 0, hp))],
        out_specs=pl.BlockSpec((None, tq, LANES), lambda bi, hp, qi: (bi, qi, hp)),
        out_shape=jax.ShapeDtypeStruct((b, t, D_ATTN), jnp.bfloat16),
        scratch_shapes=[pltpu.VMEM((2, tq, LANES), jnp.float32), pltpu.VMEM((2, tq, LANES), jnp.float32),
                        pltpu.VMEM((2, tq, tq), jnp.float32)],
        compiler_params=pltpu.CompilerParams(dimension_semantics=("arbitrary", "arbitrary", "arbitrary"),
                                             vmem_limit_bytes=VMEM_LIMIT),
        name="fox_prompt_attn",
    )(q, k, v)


def _sample_attn_kernel(pt_ref, q_ref, kn_ref, vn_ref, lfn_ref, lower_ref, lfc_hbm, kc_hbm, vc_hbm,
                        o_ref, kbuf, vbuf, lbuf, g_scr, sem, *, layer, n_pages, s_new):
    f32, bf = jnp.float32, jnp.bfloat16
    nt = (((1,), (1,)), ((), ()))
    page = kn_ref.shape[-1]
    b = pl.program_id(0)
    n_b = pl.num_programs(0)
    slot = lax.rem(b, 2)

    def page_copies(seq, slot_):
        out = []
        for p in range(n_pages):
            pg = pt_ref[seq, p]
            lanes = pl.ds(p * page, page)
            out.append(pltpu.make_async_copy(lfc_hbm.at[layer, pg], lbuf.at[slot_, :, lanes], sem.at[0, slot_]))
            out.append(pltpu.make_async_copy(kc_hbm.at[layer, pg], kbuf.at[slot_, :, :, lanes], sem.at[1, slot_]))
            out.append(pltpu.make_async_copy(vc_hbm.at[layer, pg], vbuf.at[slot_, :, :, lanes], sem.at[2, slot_]))
        return out

    @pl.when(b == 0)
    def _():
        for c in page_copies(0, 0):
            c.start()

    @pl.when(b + 1 < n_b)
    def _():
        for c in page_copies(b + 1, 1 - slot):
            c.start()

    for c in page_copies(b, slot):
        c.wait()

    carry = jnp.zeros((N_HEADS, 1), f32)
    for p in reversed(range(n_pages)):
        lf = lbuf[slot, :, p * page:(p + 1) * page]
        g_scr[:, p * page:(p + 1) * page] = (_dot_exact_lhs(lf, lower_ref[...]) + carry) * LOG2E
        carry = carry + jnp.sum(lf, axis=1, keepdims=True)

    qi = lax.broadcasted_iota(jnp.int32, (Q_ROWS, page), 0)
    ti = lax.broadcasted_iota(jnp.int32, (Q_ROWS, page), 1)
    new_ok = (ti <= qi) & (ti < s_new)
    q_row = lax.broadcasted_iota(jnp.int32, (Q_ROWS, 1), 0)
    t_lane = lax.broadcasted_iota(jnp.int32, (1, page), 1)
    for h in range(N_HEADS):
        qh = q_ref[h]
        lf_new = lfn_ref[h:h + 1, :]
        c_col = jnp.zeros((Q_ROWS, 1), f32)
        c_row = jnp.zeros((1, page), f32)
        for t_i in range(s_new):
            lf_t = lf_new[:, t_i:t_i + 1]
            c_col = c_col + jnp.where(q_row >= t_i, lf_t, 0.0)
            c_row = c_row + jnp.where(t_lane >= t_i, lf_t, 0.0)
        c_col = c_col * LOG2E
        c_row = c_row * LOG2E
        s_n = jnp.where(new_ok, _dot(qh, kn_ref[h]) + c_col - c_row, -jnp.inf)
        kt = kbuf[slot, h].astype(bf)
        vt = vbuf[slot, h].astype(bf)
        s = _dot(qh, kt) + c_col + g_scr[h:h + 1, :]
        m = jnp.maximum(jnp.max(s, axis=1, keepdims=True), jnp.max(s_n, axis=1, keepdims=True))
        pr = jnp.exp2(s - m)
        pr_n = jnp.exp2(s_n - m)
        l = jnp.sum(pr, axis=1, keepdims=True) + jnp.sum(pr_n, axis=1, keepdims=True)
        acc = (lax.dot_general(pr.astype(bf), vt, nt, preferred_element_type=f32)
               + lax.dot_general(pr_n.astype(bf), vn_ref[h], nt, preferred_element_type=f32))
        o_ref[h] = acc / l


def _sample_attention(page_table, q16, kn_t, vn_t, lf_new, lower, logf_t, cache_kt, cache_vt, layer, *, s_new):
    db, n_pages = page_table.shape
    heads, hd, page = cache_kt.shape[2:]
    past = n_pages * page
    f32 = jnp.float32

    def per_b(a):
        return pl.BlockSpec((None,) + a.shape[1:], lambda b, pt, nd=a.ndim: (b,) + (0,) * (nd - 1))

    hbm = pl.BlockSpec(memory_space=pl.ANY)
    grid_spec = pltpu.PrefetchScalarGridSpec(
        num_scalar_prefetch=1,
        grid=(db,),
        in_specs=[per_b(q16), per_b(kn_t), per_b(vn_t), per_b(lf_new),
                  pl.BlockSpec(lower.shape, lambda b, pt: (0, 0)), hbm, hbm, hbm],
        out_specs=pl.BlockSpec((None, heads, Q_ROWS, hd), lambda b, pt: (b, 0, 0, 0)),
        scratch_shapes=[pltpu.VMEM((2, heads, hd, past), f32), pltpu.VMEM((2, heads, hd, past), f32),
                        pltpu.VMEM((2, heads, past), f32), pltpu.VMEM((heads, past), f32),
                        pltpu.SemaphoreType.DMA((3, 2))],
    )
    kern = functools.partial(_sample_attn_kernel, layer=layer, n_pages=n_pages, s_new=s_new)
    return pl.pallas_call(
        kern,
        grid_spec=grid_spec,
        out_shape=jax.ShapeDtypeStruct((db, heads, Q_ROWS, hd), f32),
        compiler_params=pltpu.CompilerParams(dimension_semantics=("arbitrary",), vmem_limit_bytes=VMEM_LIMIT),
        name="fox_sample_attn",
    )(page_table, q16, kn_t, vn_t, lf_new, lower, logf_t, cache_kt, cache_vt)


def _out_kernel(x_ref, o_ref, ga_ref, sma_ref, gp_ref, gate_ref, wbra_ref, wout_ref, y_ref):
    h_a = (o_ref[...].astype(jnp.float32) * ga_ref[...].astype(jnp.float32)).astype(jnp.bfloat16)
    br_a = _dot(h_a, wbra_ref[...])
    merged = sma_ref[...].astype(jnp.float32) * br_a + gp_ref[...].astype(jnp.float32)
    y_ref[...] = x_ref[...] + gate_ref[...] * _dot(merged.astype(jnp.bfloat16), wout_ref[...])


def _out_proj(x, o, ga, sma, gp, gate, w_br_a, w_out, *, tm, tiles_per_seq):
    rows, d = x.shape

    def rowblk(width):
        return pl.BlockSpec((tm, width), lambda i: (i, 0))

    if gate.ndim == 3:
        gate_spec = pl.BlockSpec((None, 1, d), lambda i: (i // tiles_per_seq, 0, 0))
    else:
        gate_spec = pl.BlockSpec((tm, d), lambda i: (i, 0))
    return pl.pallas_call(
        _out_kernel,
        grid=(rows // tm,),
        in_specs=[rowblk(d), rowblk(D_ATTN), rowblk(D_ATTN), rowblk(d), rowblk(d), gate_spec,
                  pl.BlockSpec(w_br_a.shape, lambda i: (0, 0)), pl.BlockSpec(w_out.shape, lambda i: (0, 0))],
        out_specs=rowblk(d),
        out_shape=jax.ShapeDtypeStruct((rows, d), jnp.float32),
        compiler_params=pltpu.CompilerParams(dimension_semantics=("arbitrary",), vmem_limit_bytes=VMEM_LIMIT),
        name="out_proj",
    )(x, o, ga, sma, gp, gate, w_br_a, w_out)


def kernel(x_prompt, x_sample, cache_k, cache_v, cache_logf, state_pool, page_table, c_prompt, c_sample,
           norm_g, w_ada, b_ada, w_in, b_f, q_norm_g, k_norm_g, w_pool_grp, pool_scale, w_br_a, w_br_p, w_out):
    f32, bf = jnp.float32, jnp.bfloat16
    b, t, d = x_prompt.shape
    db, s_new, _ = x_sample.shape
    depth = w_in.shape[0]
    page = cache_k.shape[2]
    n_pages = page_table.shape[1]
    tm = 512
    tq = 512
    assert t % tm == 0 and t % tq == 0 and db % 8 == 0 and s_new <= Q_ROWS and n_pages == page_table.shape[1]

    o_q, o_k, o_v = 0, D_ATTN, 2 * D_ATTN
    o_f = 3 * D_ATTN
    o_ga = o_f + N_HEADS
    o_z = o_ga + D_ATTN
    o_gp = o_z + D_POOL
    o_m = o_gp + D_POOL
    w_f = jnp.pad(w_in[:, :, o_f:o_ga], ((0, 0), (0, 0), (0, F_PAD - N_HEADS)))
    w_in_p = jnp.concatenate([w_in[:, :, o_q:o_f], w_in[:, :, o_ga:], w_f], axis=2).astype(bf)
    b_f_p = jnp.pad(b_f, ((0, 0), (0, F_PAD - N_HEADS))).reshape(depth, 1, F_PAD)
    q_gain = jnp.tile(q_norm_g, (1, N_HEADS)).reshape(depth, 1, D_ATTN)
    k_gain = jnp.tile(k_norm_g, (1, N_HEADS)).reshape(depth, 1, D_ATTN)
    w_pool_bf = w_pool_grp.astype(bf)
    w_br_a_bf, w_br_p_bf, w_out_bf = w_br_a.astype(bf), w_br_p.astype(bf), w_out.astype(bf)

    ri = lax.broadcasted_iota(jnp.int32, (3 * LANES, N_HEADS * LANES), 0)
    ci = lax.broadcasted_iota(jnp.int32, (3 * LANES, N_HEADS * LANES), 1)
    piece, src_head = ri // LANES, ri % LANES
    off = ci % LANES - HEAD_DIM
    consts = {"place": ((src_head == ci // LANES) & ((off == piece) | (off == piece + 3))).astype(bf)}
    rt = lax.broadcasted_iota(jnp.int32, (tm, tm), 0)
    ct = lax.broadcasted_iota(jnp.int32, (tm, tm), 1)
    consts["tri"] = (ct <= rt).astype(bf)
    rp = lax.broadcasted_iota(jnp.int32, (page, page), 0)
    cp = lax.broadcasted_iota(jnp.int32, (page, page), 1)
    lower = (rp > cp).astype(bf)
    cache_kt = jnp.transpose(cache_k, (0, 1, 3, 4, 2))
    cache_vt = jnp.transpose(cache_v, (0, 1, 3, 4, 2))
    logf_t = jnp.transpose(cache_logf, (0, 1, 3, 2))

    mc = db + 8
    c_all = jnp.concatenate([c_sample, c_prompt, jnp.zeros((mc - db - b, d), f32)], axis=0)
    mod = _modulation(c_all, w_ada, b_ada)

    hp = x_prompt.reshape(b * t, d)
    hs = jnp.transpose(x_sample, (1, 0, 2)).reshape(s_new * db, d)
    hist_zero = jnp.zeros((POOL_SLOTS, D_POOL), f32)
    tiles_per_seq = t // tm

    outs = {n: [] for n in ("kp", "vp", "fp", "pp", "ks", "vs", "fs", "ps")}
    for l in range(depth):
        lw = {"norm_g": norm_g[l].reshape(1, d), "w_in": w_in_p[l], "b_f": b_f_p[l], "q_gain": q_gain[l],
              "k_gain": k_gain[l], "w_pool": w_pool_bf[l], "pool_scale": pool_scale[l].reshape(1, D_POOL),
              "w_br_p": w_br_p_bf[l]}
        mod_s = mod[l, :db]
        mod_p = mod[l, db:db + b]
        shift_p, scale_p, gate_p = (mod_p[:, j * d:(j + 1) * d].reshape(b, 1, d) for j in range(3))
        mod_s_rows = jnp.tile(mod_s, (s_new, 1))
        shift_s, scale_s, gate_s = (mod_s_rows[:, j * d:(j + 1) * d] for j in range(3))

        (q_p, k_stack, kb_p, v_stack, vb_p, lf_p, ga_p, sma_p, gp_p, tail_p) = _inproj(
            hp, scale_p, shift_p, hist_zero, lw, consts, tm=tm, tstride=1, tiles_per_seq=tiles_per_seq,
            prompt=True, stacked=(l, depth), kv_bufs=() if l == 0 else (k_stack, v_stack))
        w_aug = N_HEADS * LANES
        o_p = _prompt_attention(q_p.reshape(b, t, w_aug), kb_p.reshape(b, t, w_aug), vb_p.reshape(b, t, w_aug),
                                tq=tq)
        hp = _out_proj(hp, o_p.reshape(b * t, D_ATTN), ga_p, sma_p, gp_p, gate_p, w_br_a_bf[l], w_out_bf[l],
                       tm=tm, tiles_per_seq=tiles_per_seq)
        outs["fp"].append(lf_p.reshape(b, t, N_HEADS))
        outs["pp"].append(tail_p[:, 1:, :])

        hist_s = jnp.concatenate([jnp.zeros((1, db, D_POOL), f32), jnp.transpose(state_pool[l], (1, 0, 2))],
                                 axis=0).reshape(POOL_SLOTS * db, D_POOL)
        (q_s, k32_s, kb_s, v32_s, vb_s, lf_s, ga_s, sma_s, gp_s, tail_s) = _inproj(
            hs, scale_s, shift_s, hist_s, lw, consts, tm=s_new * db, tstride=db, tiles_per_seq=1,
            prompt=False)
        q16 = jnp.transpose(q_s.reshape(s_new, db, N_HEADS, HEAD_DIM), (1, 2, 0, 3))
        q16 = jnp.pad(q16, ((0, 0), (0, 0), (0, Q_ROWS - s_new), (0, 0)))
        kn_t = jnp.transpose(kb_s.reshape(s_new, db, N_HEADS, HEAD_DIM), (1, 2, 3, 0))
        kn_t = jnp.pad(kn_t, ((0, 0), (0, 0), (0, 0), (0, page - s_new)))
        vn_t = jnp.transpose(vb_s.reshape(s_new, db, N_HEADS, HEAD_DIM), (1, 2, 3, 0))
        vn_t = jnp.pad(vn_t, ((0, 0), (0, 0), (0, 0), (0, page - s_new)))
        lf_bt = jnp.transpose(lf_s.reshape(s_new, db, N_HEADS), (1, 0, 2))
        lf_bh = jnp.transpose(lf_bt, (0, 2, 1))
        lf_new = jnp.pad(lf_bh, ((0, 0), (0, 0), (0, page - s_new)))
        o_s = _sample_attention(page_table, q16, kn_t, vn_t, lf_new, lower, logf_t, cache_kt, cache_vt, l,
                                s_new=s_new)
        o_s = jnp.transpose(o_s[:, :, :s_new, :], (2, 0, 1, 3)).reshape(s_new * db, D_ATTN).astype(bf)
        hs = _out_proj(hs, o_s, ga_s, sma_s, gp_s, gate_s, w_br_a_bf[l], w_out_bf[l], tm=s_new * db,
                       tiles_per_seq=1)
        outs["ks"].append(jnp.transpose(k32_s.reshape(s_new, db, N_HEADS, HEAD_DIM), (1, 0, 2, 3)))
        outs["vs"].append(jnp.transpose(v32_s.reshape(s_new, db, N_HEADS, HEAD_DIM), (1, 0, 2, 3)))
        outs["fs"].append(lf_bt)
        outs["ps"].append(jnp.transpose(tail_s.reshape(POOL_SLOTS, db, D_POOL)[1:], (1, 0, 2)))

    y_p = hp.reshape(b, t, d)
    y_s = jnp.transpose(hs.reshape(s_new, db, d), (1, 0, 2))
    k_p = jnp.transpose(k_stack.reshape(depth, b, N_HEADS, HEAD_DIM, t), (0, 1, 4, 2, 3))
    v_p = jnp.transpose(v_stack.reshape(depth, b, N_HEADS, HEAD_DIM, t), (0, 1, 4, 2, 3))
    return (y_p, y_s, k_p, v_p, jnp.stack(outs["fp"]),
            jnp.stack(outs["pp"]), jnp.stack(outs["ks"]), jnp.stack(outs["vs"]), jnp.stack(outs["fs"]),
            jnp.stack(outs["ps"]))
```

```python
import functools
import math

import jax
import jax.numpy as jnp
from jax import lax
from jax.experimental import pallas as pl
from jax.experimental.pallas import tpu as pltpu

N_HEADS = 8
HEAD_DIM = 64
D_ATTN = N_HEADS * HEAD_DIM
POOL_WINDOWS = (2, 4, 8, 16)
POOL_GC = 128
D_POOL = POOL_GC * len(POOL_WINDOWS)
POOL_HIST = max(POOL_WINDOWS) - 1
POOL_SLOTS = POOL_HIST + 1
EPS = 1e-6
LOG2E = math.log2(math.e)
LANES = 128
F_PAD = LANES
Q_ROWS = 16
VMEM_LIMIT = 56 * 1024 * 1024

C_Q, C_K, C_V, C_GA, C_Z, C_GP = 0, 512, 1024, 1536, 2048, 2560
C_MA = 3072


def _split3(x):
    hi = x.astype(jnp.bfloat16)
    r1 = x - hi.astype(jnp.float32)
    mid = r1.astype(jnp.bfloat16)
    lo = (r1 - mid.astype(jnp.float32)).astype(jnp.bfloat16)
    return hi, mid, lo


def _dot(a, b):
    return jnp.dot(a, b, preferred_element_type=jnp.float32)


def _dot_exact_rhs(sel, x):
    hi, mid, lo = _split3(x)
    return _dot(sel, hi) + _dot(sel, mid) + _dot(sel, lo)


def _dot_exact_lhs(x, sel):
    hi, mid, lo = _split3(x)
    return _dot(hi, sel) + _dot(mid, sel) + _dot(lo, sel)


def _div_pow2(x, n):
    assert n & (n - 1) == 0
    return lax.shift_right_logical(x, n.bit_length() - 1)


def _mod_pow2(x, n):
    assert n & (n - 1) == 0
    return lax.bitwise_and(x, n - 1)


def _silu(x):
    return x * jax.nn.sigmoid(x)


def _log_sigmoid(x):
    return jnp.minimum(x, 0.0) - jnp.log1p(jnp.exp(-jnp.abs(x)))


def _mod_kernel(c_ref, w_ref, b_ref, o_ref):
    a_hi, a_mid, a_lo = _split3(_silu(c_ref[...]))
    w_hi, w_mid, w_lo = _split3(w_ref[...])
    acc = _dot(a_hi, w_hi) + (_dot(a_hi, w_mid) + _dot(a_mid, w_hi))
    acc = acc + (_dot(a_hi, w_lo) + _dot(a_lo, w_hi) + _dot(a_mid, w_mid))
    o_ref[...] = acc + b_ref[...]


def _modulation(c_all, w_ada_bf, b_ada):
    depth, d, d3 = w_ada_bf.shape
    mc = c_all.shape[0]
    tn = 512
    return pl.pallas_call(
        _mod_kernel,
        grid=(depth, d3 // tn),
        in_specs=[pl.BlockSpec((mc, d), lambda l, j: (0, 0)),
                  pl.BlockSpec((None, d, tn), lambda l, j: (l, 0, j)),
                  pl.BlockSpec((None, 1, tn), lambda l, j: (l, 0, j))],
        out_specs=pl.BlockSpec((None, mc, tn), lambda l, j: (l, 0, j)),
        out_shape=jax.ShapeDtypeStruct((depth, mc, d3), jnp.float32),
        name="adaln_mod",
    )(c_all, w_ada_bf, b_ada.reshape(depth, 1, d3))


N_INPROJ_IN = 12


def _cumsum_rows(x):
    n = x.shape[0]
    row = lax.broadcasted_iota(jnp.int32, x.shape, 0)
    step = 1
    while step < n:
        x = x + jnp.where(row >= step, pltpu.roll(x, step, axis=0), 0.0)
        step *= 2
    return x


def _inproj_kernel(*refs, tm, tstride, tiles_per_seq, prompt, n_alias):
    (x_ref, scale_ref, shift_ref, g_ref, w_ref, bf_ref, qg_ref, kg_ref,
     hist_ref, wpool_ref, pscale_ref, wbrp_ref) = refs[:N_INPROJ_IN]
    (q_ref, k32_ref, kb_ref, v32_ref, vb_ref, logf_ref, ga_ref, sma_ref, gp_ref, tail_ref,
     zext_scr, fcarry_scr) = refs[N_INPROJ_IN + n_alias:]
    hist_rows = POOL_SLOTS * tstride
    i = pl.program_id(0)
    tile_in_seq = lax.rem(i, tiles_per_seq)
    first = tile_in_seq == 0

    @pl.when(first)
    def _():
        zext_scr[0:hist_rows, :] = hist_ref[...]
        fcarry_scr[...] = jnp.zeros_like(fcarry_scr)

    x = x_ref[...]
    ms = jnp.mean(x * x, axis=-1, keepdims=True)
    xn = x * lax.rsqrt(ms + EPS) * g_ref[...]
    u = (xn * (1.0 + scale_ref[...]) + shift_ref[...]).astype(jnp.bfloat16)

    def proj(c0, width):
        return _dot(u, w_ref[:, c0:c0 + width])

    lane = lax.broadcasted_iota(jnp.int32, (tm, LANES), 1)
    low_half = lane < HEAD_DIM

    def head_norm(p, gain_ref):
        outs = []
        for j in range(D_ATTN // LANES):
            slab = p[:, j * LANES:(j + 1) * LANES]
            sq = slab * slab
            s_lo = jnp.sum(jnp.where(low_half, sq, 0.0), axis=1, keepdims=True)
            s_hi = jnp.sum(jnp.where(low_half, 0.0, sq), axis=1, keepdims=True)
            r = jnp.where(low_half, lax.rsqrt(s_lo * (1.0 / HEAD_DIM) + EPS), lax.rsqrt(s_hi * (1.0 / HEAD_DIM) + EPS))
            outs.append(slab * r)
        return jnp.concatenate(outs, axis=1) * gain_ref[...]

    c_f = w_ref.shape[1] - F_PAD
    logf = _log_sigmoid(proj(c_f, F_PAD) + bf_ref[...])
    logf_ref[...] = logf[:, :N_HEADS]

    qn = head_norm(proj(C_Q, D_ATTN), qg_ref) * (HEAD_DIM ** -0.5 * LOG2E)
    kn = head_norm(proj(C_K, D_ATTN), kg_ref)
    pv = proj(C_V, D_ATTN)
    k32_ref[...] = kn.T if prompt else kn
    v32_ref[...] = pv.T if prompt else pv
    if prompt:
        fc = _cumsum_rows(logf) + fcarry_scr[...]
        fcarry_scr[...] = fc[tm - 1:tm, :]
        pieces = [p.astype(jnp.float32) for p in _split3(fc * LOG2E)]
        for h in range(N_HEADS):
            j = h // 2
            f_hi, f_mid, f_lo = (p[:, h:h + 1] for p in pieces)

            def head_tile(x, h=h, j=j):
                slab = x[:, j * LANES:(j + 1) * LANES]
                return slab if h % 2 == 0 else pltpu.roll(slab, HEAD_DIM, axis=1)

            def tail(a, b, c, rest, first):
                return jnp.where(lane == first, a, jnp.where(lane == first + 1, b,
                                                             jnp.where(lane == first + 2, c, rest)))

            ones_q = jnp.where(lane < HEAD_DIM + 3, 1.0, 0.0)
            ones_k = jnp.where((lane >= HEAD_DIM + 3) & (lane < HEAD_DIM + 6), 1.0, 0.0)
            q_t = jnp.where(low_half, head_tile(qn), tail(f_hi, f_mid, f_lo, ones_q, HEAD_DIM + 3))
            k_t = jnp.where(low_half, head_tile(kn), tail(-f_hi, -f_mid, -f_lo, ones_k, HEAD_DIM))
            v_t = jnp.where(low_half, head_tile(pv), jnp.where(lane == HEAD_DIM, 1.0, 0.0))
            q_ref[:, h * LANES:(h + 1) * LANES] = q_t.astype(jnp.bfloat16)
            kb_ref[:, h * LANES:(h + 1) * LANES] = k_t.astype(jnp.bfloat16)
            vb_ref[:, h * LANES:(h + 1) * LANES] = v_t.astype(jnp.bfloat16)
    else:
        q_ref[...] = qn.astype(jnp.bfloat16)
        kb_ref[...] = kn.astype(jnp.bfloat16)
        vb_ref[...] = pv.astype(jnp.bfloat16)

    ga_ref[...] = _silu(proj(C_GA, D_ATTN)).astype(jnp.bfloat16)

    zext_scr[hist_rows:hist_rows + tm, :] = proj(C_Z, D_POOL)
    if prompt:
        pos = tile_in_seq * tm + lax.broadcasted_iota(jnp.int32, (tm, 1), 0)
    ys = []
    for g, w in enumerate(POOL_WINDOWS):
        col = zext_scr[:, g * POOL_GC:(g + 1) * POOL_GC]
        s = col
        span = 1
        while span < w:
            s = s + pltpu.roll(s, span * tstride, axis=0)
            span *= 2
        if prompt:
            cnt = jnp.minimum(pos + 1, w).astype(jnp.float32)
        else:
            cnt = jnp.float32(w)
        diff = s[hist_rows:, :] / cnt - col[hist_rows:, :]
        ys.append(_dot(diff.astype(jnp.bfloat16), wpool_ref[g]))
    y_pool = jnp.concatenate(ys, axis=1) * pscale_ref[...]
    tail = zext_scr[tm:tm + hist_rows, :]
    tail_ref[...] = tail
    if tiles_per_seq > 1:
        zext_scr[0:hist_rows, :] = tail

    h_p = (y_pool * _silu(proj(C_GP, D_POOL))).astype(jnp.bfloat16)
    d_model = x.shape[1]
    half = d_model // 2
    for c in range(2):
        br_p = _dot(h_p, wbrp_ref[:, c * half:(c + 1) * half])
        m_p = proj(C_MA + d_model + c * half, half)
        gp_ref[:, c * half:(c + 1) * half] = (jax.nn.sigmoid(m_p) * br_p).astype(jnp.bfloat16)
        m_a = proj(C_MA + c * half, half)
        sma_ref[:, c * half:(c + 1) * half] = jax.nn.sigmoid(m_a).astype(jnp.bfloat16)


def _inproj(x, scale, shift, hist, lw, *, tm, tstride, tiles_per_seq, prompt, stacked=None, kv_bufs=()):
    rows, d = x.shape
    n_tiles = rows // tm
    hist_rows = POOL_SLOTS * tstride
    w_in = lw["w_in"]
    bf, f32 = jnp.bfloat16, jnp.float32

    def full(a):
        return pl.BlockSpec(a.shape, lambda i, nd=a.ndim: (0,) * nd)

    if scale.ndim == 3:
        mod_spec = pl.BlockSpec((None, 1, d), lambda i: (i // tiles_per_seq, 0, 0))
    else:
        mod_spec = pl.BlockSpec((tm, d), lambda i: (i, 0))

    def rowblk(width):
        return pl.BlockSpec((tm, width), lambda i: (i, 0))

    n_seq = n_tiles // tiles_per_seq
    w_att = N_HEADS * LANES if prompt else D_ATTN
    if prompt:
        layer, depth = stacked
        kv_shape = jax.ShapeDtypeStruct((depth, n_seq, D_ATTN, tiles_per_seq * tm), f32)
        kv_spec = pl.BlockSpec((None, None, D_ATTN, tm),
                               lambda i: (layer, i // tiles_per_seq, 0, lax.rem(i, tiles_per_seq)))
    else:
        kv_shape = jax.ShapeDtypeStruct((rows, D_ATTN), f32)
        kv_spec = rowblk(D_ATTN)
    n_alias = len(kv_bufs)
    out_shape = (
        jax.ShapeDtypeStruct((rows, w_att), bf),
        kv_shape,
        jax.ShapeDtypeStruct((rows, w_att), bf),
        kv_shape,
        jax.ShapeDtypeStruct((rows, w_att), bf),
        jax.ShapeDtypeStruct((rows, N_HEADS), f32),
        jax.ShapeDtypeStruct((rows, D_ATTN), bf),
        jax.ShapeDtypeStruct((rows, d), bf),
        jax.ShapeDtypeStruct((rows, d), bf),
        jax.ShapeDtypeStruct((n_seq, hist_rows, D_POOL), f32),
    )
    out_specs = (rowblk(w_att), kv_spec, rowblk(w_att), kv_spec, rowblk(w_att),
                 rowblk(N_HEADS), rowblk(D_ATTN), rowblk(d), rowblk(d),
                 pl.BlockSpec((None, hist_rows, D_POOL), lambda i: (i // tiles_per_seq, 0, 0)))
    kern = functools.partial(_inproj_kernel, tm=tm, tstride=tstride, tiles_per_seq=tiles_per_seq,
                             prompt=prompt, n_alias=n_alias)
    operands = (x, scale, shift, lw["norm_g"], w_in, lw["b_f"], lw["q_gain"], lw["k_gain"],
                hist, lw["w_pool"], lw["pool_scale"], lw["w_br_p"])
    assert len(operands) == N_INPROJ_IN
    return pl.pallas_call(
        kern,
        grid=(n_tiles,),
        in_specs=[rowblk(d), mod_spec, mod_spec, full(lw["norm_g"]), full(w_in), full(lw["b_f"]),
                  full(lw["q_gain"]), full(lw["k_gain"]), full(hist), full(lw["w_pool"]), full(lw["pool_scale"]), full(lw["w_br_p"])]
                 + [pl.BlockSpec(memory_space=pl.ANY)] * n_alias,
        out_specs=out_specs,
        out_shape=out_shape,
        input_output_aliases={N_INPROJ_IN: 1, N_INPROJ_IN + 1: 3} if n_alias else {},
        scratch_shapes=[pltpu.VMEM((hist_rows + tm, D_POOL), f32), pltpu.VMEM((1, F_PAD), f32)],
        compiler_params=pltpu.CompilerParams(dimension_semantics=("arbitrary",), vmem_limit_bytes=VMEM_LIMIT),
        name="in_proj",
    )(*operands, *kv_bufs)


def _attn_kernel(q_ref, k_ref, v_ref, o_ref, m_scr, acc_scr, s_scr, *, tq, hg):
    qi = pl.program_id(2)
    row = lax.broadcasted_iota(jnp.int32, (tq, tq), 0)
    col = lax.broadcasted_iota(jnp.int32, (tq, tq), 1)
    reps = tq // LANES
    nt = (((1,), (1,)), ((), ()))
    m_scr[...] = jnp.full_like(m_scr, -jnp.inf)
    acc_scr[...] = jnp.zeros_like(acc_scr)

    def scores(kt, hh):
        start = pl.multiple_of(kt * tq, tq)
        sl = slice(hh * LANES, (hh + 1) * LANES)
        return lax.dot_general(q_ref[:, sl], k_ref[pl.ds(start, tq), sl], nt, preferred_element_type=jnp.float32)

    def accumulate(s, kt, hh, masked):
        start = pl.multiple_of(kt * tq, tq)
        sl = slice(hh * LANES, (hh + 1) * LANES)
        if masked:
            s = jnp.where(col <= row, s, -jnp.inf)
        m_prev = m_scr[hh]
        m_next = jnp.maximum(m_prev, jnp.max(s, axis=1, keepdims=True))
        p = jnp.exp2(s - jnp.tile(m_next, (1, reps)))
        alpha = jnp.exp2(m_prev - m_next)
        acc_scr[hh] = alpha * acc_scr[hh] + _dot(p.astype(jnp.bfloat16), v_ref[pl.ds(start, tq), sl])
        m_scr[hh] = m_next

    for hh in range(hg):
        s_scr[hh] = scores(0, hh)

    def body(kt, carry):
        for hh in range(hg):
            s = s_scr[hh]
            s_scr[hh] = scores(kt + 1, hh)
            accumulate(s, kt, hh, False)
        return carry

    lax.fori_loop(0, qi, body, 0)
    for hh in range(hg):
        accumulate(s_scr[hh], qi, hh, True)
    lane = lax.broadcasted_iota(jnp.int32, (tq, LANES), 1)
    for j in range(hg // 2):
        a0, a1 = acc_scr[2 * j], acc_scr[2 * j + 1]
        o0 = a0 / a0[:, HEAD_DIM:HEAD_DIM + 1]
        o1 = a1 / a1[:, HEAD_DIM:HEAD_DIM + 1]
        o_ref[:, j * LANES:(j + 1) * LANES] = jnp.where(lane < HEAD_DIM, o0,
                                                        pltpu.roll(o1, HEAD_DIM, axis=1)).astype(o_ref.dtype)


def _prompt_attention(q, k, v, *, tq, hg):
    b, t, _ = q.shape
    n_t = t // tq
    group = hg * LANES
    kern = functools.partial(_attn_kernel, tq=tq, hg=hg)
    return pl.pallas_call(
        kern,
        grid=(b, N_HEADS // hg, n_t),
        in_specs=[pl.BlockSpec((None, tq, group), lambda bi, hp, qi: (bi, qi, hp)),
                  pl.BlockSpec((None, t, group), lambda bi, hp, qi: (bi, 0, hp)),
                  pl.BlockSpec((None, t, group), lambda bi, hp, qi: (bi, 0, hp))],
        out_specs=pl.BlockSpec((None, tq, hg * HEAD_DIM), lambda bi, hp, qi: (bi, qi, hp)),
        out_shape=jax.ShapeDtypeStruct((b, t, D_ATTN), jnp.bfloat16),
        scratch_shapes=[pltpu.VMEM((hg, tq, LANES), jnp.float32), pltpu.VMEM((hg, tq, LANES), jnp.float32),
                        pltpu.VMEM((hg, tq, tq), jnp.float32)],
        compiler_params=pltpu.CompilerParams(dimension_semantics=("arbitrary", "arbitrary", "arbitrary"),
                                             vmem_limit_bytes=VMEM_LIMIT),
        name="fox_prompt_attn",
    )(q, k, v)


def _sample_attn_kernel(pt_ref, q_ref, kn_ref, vn_ref, lfn_ref, lower_ref, lfc_hbm, kc_hbm, vc_hbm,
                        o_ref, kbuf, vbuf, lbuf, g_scr, sem, *, layer, n_pages, s_new):
    f32, bf = jnp.float32, jnp.bfloat16
    nt = (((1,), (1,)), ((), ()))
    page = kn_ref.shape[-1]
    b = pl.program_id(0)
    n_b = pl.num_programs(0)
    slot = lax.rem(b, 2)

    def page_copies(seq, slot_):
        out = []
        for p in range(n_pages):
            pg = pt_ref[seq, p]
            lanes = pl.ds(p * page, page)
            out.append(pltpu.make_async_copy(lfc_hbm.at[layer, pg], lbuf.at[slot_, :, lanes], sem.at[0, slot_]))
            out.append(pltpu.make_async_copy(kc_hbm.at[layer, pg], kbuf.at[slot_, :, :, lanes], sem.at[1, slot_]))
            out.append(pltpu.make_async_copy(vc_hbm.at[layer, pg], vbuf.at[slot_, :, :, lanes], sem.at[2, slot_]))
        return out

    @pl.when(b == 0)
    def _():
        for c in page_copies(0, 0):
            c.start()

    @pl.when(b + 1 < n_b)
    def _():
        for c in page_copies(b + 1, 1 - slot):
            c.start()

    for c in page_copies(b, slot):
        c.wait()

    carry = jnp.zeros((N_HEADS, 1), f32)
    for p in reversed(range(n_pages)):
        lf = lbuf[slot, :, p * page:(p + 1) * page]
        g_scr[:, p * page:(p + 1) * page] = (_dot_exact_lhs(lf, lower_ref[...]) + carry) * LOG2E
        carry = carry + jnp.sum(lf, axis=1, keepdims=True)

    qi = lax.broadcasted_iota(jnp.int32, (Q_ROWS, page), 0)
    ti = lax.broadcasted_iota(jnp.int32, (Q_ROWS, page), 1)
    new_ok = (ti <= qi) & (ti < s_new)
    q_row =lax.broadcasted_iota(jnp.int32, (Q_ROWS, 1), 0)
    t_lane = lax.broadcasted_iota(jnp.int32, (1, page), 1)
    for h in range(N_HEADS):
        qh = q_ref[h]
        lf_new = lfn_ref[h:h + 1, :]
        c_col = jnp.zeros((Q_ROWS, 1), f32)
        c_row = jnp.zeros((1, page), f32)
        for t_i in range(s_new):
            lf_t = lf_new[:, t_i:t_i + 1]
            c_col = c_col + jnp.where(q_row >= t_i, lf_t, 0.0)
            c_row = c_row + jnp.where(t_lane >= t_i, lf_t, 0.0)
        c_col = c_col * LOG2E
        c_row = c_row * LOG2E
        s_n = jnp.where(new_ok, _dot(qh, kn_ref[h]) + c_col - c_row, -jnp.inf)
        kt = kbuf[slot, h].astype(bf)
        vt = vbuf[slot, h].astype(bf)
        s = _dot(qh, kt) + c_col + g_scr[h:h + 1, :]
        m = jnp.maximum(jnp.max(s, axis=1, keepdims=True), jnp.max(s_n, axis=1, keepdims=True))
        pr = jnp.exp2(s - m)
        pr_n = jnp.exp2(s_n - m)
        l = jnp.sum(pr, axis=1, keepdims=True) + jnp.sum(pr_n, axis=1, keepdims=True)
        acc = (lax.dot_general(pr.astype(bf), vt, nt, preferred_element_type=f32)
               + lax.dot_general(pr_n.astype(bf), vn_ref[h], nt, preferred_element_type=f32))
        o_ref[h] = acc / l


def _sample_attention(page_table, q16, kn_t, vn_t, lf_new, lower, logf_t, cache_kt, cache_vt, layer, *, s_new):
    db, n_pages = page_table.shape
    heads, hd, page = cache_kt.shape[2:]
    past = n_pages * page
    f32 = jnp.float32

    def per_b(a):
        return pl.BlockSpec((None,) + a.shape[1:], lambda b, pt, nd=a.ndim: (b,) + (0,) * (nd - 1))

    hbm = pl.BlockSpec(memory_space=pl.ANY)
    grid_spec = pltpu.PrefetchScalarGridSpec(
        num_scalar_prefetch=1,
        grid=(db,),
        in_specs=[per_b(q16), per_b(kn_t), per_b(vn_t), per_b(lf_new),
                  pl.BlockSpec(lower.shape, lambda b, pt: (0, 0)), hbm, hbm, hbm],
        out_specs=pl.BlockSpec((None, heads, Q_ROWS, hd), lambda b, pt: (b, 0, 0, 0)),
        scratch_shapes=[pltpu.VMEM((2, heads, hd, past), f32), pltpu.VMEM((2, heads, hd, past), f32),
                        pltpu.VMEM((2, heads, past), f32), pltpu.VMEM((heads, past), f32),
                        pltpu.SemaphoreType.DMA((3, 2))],
    )
    kern = functools.partial(_sample_attn_kernel, layer=layer, n_pages=n_pages, s_new=s_new)
    return pl.pallas_call(
        kern,
        grid_spec=grid_spec,
        out_shape=jax.ShapeDtypeStruct((db, heads, Q_ROWS, hd), f32),
        compiler_params=pltpu.CompilerParams(dimension_semantics=("arbitrary",), vmem_limit_bytes=VMEM_LIMIT),
        name="fox_sample_attn",
    )(page_table, q16, kn_t, vn_t, lf_new, lower, logf_t, cache_kt, cache_vt)


def _out_kernel(x_ref, o_ref, ga_ref, sma_ref, gp_ref, gate_ref, wbra_ref, wout_ref, y_ref):
    h_a = (o_ref[...].astype(jnp.float32) * ga_ref[...].astype(jnp.float32)).astype(jnp.bfloat16)
    br_a = _dot(h_a, wbra_ref[...])
    merged = sma_ref[...].astype(jnp.float32) * br_a + gp_ref[...].astype(jnp.float32)
    y_ref[...] = x_ref[...] + gate_ref[...] * _dot(merged.astype(jnp.bfloat16), wout_ref[...])


def _out_proj(x, o, ga, sma, gp, gate, w_br_a, w_out, *, tm, tiles_per_seq):
    rows, d = x.shape

    def rowblk(width):
        return pl.BlockSpec((tm, width), lambda i: (i, 0))

    if gate.ndim == 3:
        gate_spec = pl.BlockSpec((None, 1, d), lambda i: (i // tiles_per_seq, 0, 0))
    else:
        gate_spec = pl.BlockSpec((tm, d), lambda i: (i, 0))
    return pl.pallas_call(
        _out_kernel,
        grid=(rows // tm,),
        in_specs=[rowblk(d), rowblk(D_ATTN), rowblk(D_ATTN), rowblk(d), rowblk(d), gate_spec,
                  pl.BlockSpec(w_br_a.shape, lambda i: (0, 0)), pl.BlockSpec(w_out.shape, lambda i: (0, 0))],
        out_specs=rowblk(d),
        out_shape=jax.ShapeDtypeStruct((rows, d), jnp.float32),
        compiler_params=pltpu.CompilerParams(dimension_semantics=("arbitrary",), vmem_limit_bytes=VMEM_LIMIT),
        name="out_proj",
    )(x, o, ga, sma, gp, gate, w_br_a, w_out)


def kernel(x_prompt, x_sample, cache_k, cache_v, cache_logf, state_pool, page_table, c_prompt, c_sample,
           norm_g, w_ada, b_ada, w_in, b_f, q_norm_g, k_norm_g, w_pool_grp, pool_scale, w_br_a, w_br_p, w_out):
    f32, bf = jnp.float32, jnp.bfloat16
    b, t, d = x_prompt.shape
    db, s_new, _ = x_sample.shape
    depth = w_in.shape[0]
    page = cache_k.shape[2]
    n_pages = page_table.shape[1]
    tm = 512
    tq = 512
    assert t % tm == 0 and t % tq == 0 and db % 8 == 0 and s_new <= Q_ROWS and n_pages == page_table.shape[1]

    o_q, o_k, o_v = 0, D_ATTN, 2 * D_ATTN
    o_f = 3 * D_ATTN
    o_ga = o_f + N_HEADS
    o_z = o_ga + D_ATTN
    o_gp = o_z + D_POOL
    o_m = o_gp + D_POOL
    w_f = jnp.pad(w_in[:, :, o_f:o_ga], ((0, 0), (0, 0), (0, F_PAD - N_HEADS)))
    w_in_p = jnp.concatenate([w_in[:, :, o_q:o_f], w_in[:, :, o_ga:], w_f], axis=2).astype(bf)
    b_f_p = jnp.pad(b_f, ((0, 0), (0, F_PAD - N_HEADS))).reshape(depth, 1, F_PAD)
    q_gain = jnp.tile(q_norm_g, (1, N_HEADS)).reshape(depth, 1, D_ATTN)
    k_gain = jnp.tile(k_norm_g, (1, N_HEADS)).reshape(depth, 1, D_ATTN)
    w_pool_bf = w_pool_grp.astype(bf)
    w_br_a_bf, w_br_p_bf, w_out_bf = w_br_a.astype(bf), w_br_p.astype(bf), w_out.astype(bf)

    rp =lax.broadcasted_iota(jnp.int32, (page, page), 0)
    cp = lax.broadcasted_iota(jnp.int32, (page, page), 1)
    lower = (rp > cp).astype(bf)
    cache_kt = jnp.transpose(cache_k, (0, 1, 3, 4, 2))
    cache_vt = jnp.transpose(cache_v, (0, 1, 3, 4, 2))
    logf_t = jnp.transpose(cache_logf, (0, 1, 3, 2))

    mc = db + 8
    c_all = jnp.concatenate([c_sample, c_prompt, jnp.zeros((mc - db - b, d), f32)], axis=0)
    mod = _modulation(c_all, w_ada, b_ada)

    hp = x_prompt.reshape(b * t, d)
    hs = jnp.transpose(x_sample, (1, 0, 2)).reshape(s_new * db, d)
    hist_zero = jnp.zeros((POOL_SLOTS, D_POOL), f32)
    tiles_per_seq = t // tm

    outs = {n: [] for n in ("kp", "vp", "fp", "pp", "ks", "vs", "fs", "ps")}
    for l in range(depth):
        lw = {"norm_g": norm_g[l].reshape(1, d), "w_in": w_in_p[l], "b_f": b_f_p[l], "q_gain": q_gain[l],
              "k_gain": k_gain[l], "w_pool": w_pool_bf[l], "pool_scale": pool_scale[l].reshape(1, D_POOL),
              "w_br_p": w_br_p_bf[l]}
        mod_s = mod[l, :db]
        mod_p = mod[l, db:db + b]
        shift_p, scale_p, gate_p = (mod_p[:, j * d:(j + 1) * d].reshape(b, 1, d) for j in range(3))
        mod_s_rows = jnp.tile(mod_s, (s_new, 1))
        shift_s, scale_s, gate_s = (mod_s_rows[:, j * d:(j + 1) * d] for j in range(3))

        (q_p, k_stack, kb_p, v_stack, vb_p, lf_p, ga_p, sma_p, gp_p, tail_p) = _inproj(
            hp, scale_p, shift_p, hist_zero, lw, tm=tm, tstride=1, tiles_per_seq=tiles_per_seq,
            prompt=True, stacked=(l, depth), kv_bufs=() if l == 0 else (k_stack, v_stack))
        w_aug = N_HEADS * LANES
        o_p = _prompt_attention(q_p.reshape(b, t, w_aug), kb_p.reshape(b, t, w_aug), vb_p.reshape(b, t, w_aug),
                                tq=tq, hg=4)
        hp = _out_proj(hp, o_p.reshape(b * t, D_ATTN), ga_p, sma_p, gp_p, gate_p, w_br_a_bf[l], w_out_bf[l],
                       tm=tm, tiles_per_seq=tiles_per_seq)
        outs["fp"].append(lf_p.reshape(b, t, N_HEADS))
        outs["pp"].append(tail_p[:, 1:, :])

        hist_s = jnp.concatenate([jnp.zeros((1, db, D_POOL), f32), jnp.transpose(state_pool[l], (1, 0, 2))],
                                 axis=0).reshape(POOL_SLOTS * db, D_POOL)
        (q_s, k32_s, kb_s, v32_s, vb_s, lf_s, ga_s, sma_s, gp_s, tail_s) = _inproj(
            hs, scale_s, shift_s, hist_s, lw, tm=s_new * db, tstride=db, tiles_per_seq=1,
            prompt=False)
        q16 = jnp.transpose(q_s.reshape(s_new, db, N_HEADS, HEAD_DIM), (1, 2, 0, 3))
        q16 = jnp.pad(q16, ((0, 0), (0, 0), (0, Q_ROWS - s_new), (0, 0)))
        kn_t = jnp.transpose(kb_s.reshape(s_new, db, N_HEADS, HEAD_DIM), (1, 2, 3, 0))
        kn_t = jnp.pad(kn_t, ((0, 0), (0, 0), (0, 0), (0, page - s_new)))
        vn_t = jnp.transpose(vb_s.reshape(s_new, db, N_HEADS, HEAD_DIM), (1, 2, 3, 0))
        vn_t = jnp.pad(vn_t, ((0, 0), (0, 0), (0, 0), (0, page - s_new)))
        lf_bt = jnp.transpose(lf_s.reshape(s_new, db, N_HEADS), (1, 0, 2))
        lf_bh = jnp.transpose(lf_bt, (0, 2, 1))
        lf_new = jnp.pad(lf_bh, ((0, 0), (0, 0), (0, page - s_new)))
        o_s = _sample_attention(page_table, q16, kn_t, vn_t, lf_new, lower, logf_t, cache_kt, cache_vt, l,
                                s_new=s_new)
        o_s = jnp.transpose(o_s[:, :, :s_new, :], (2, 0, 1, 3)).reshape(s_new * db, D_ATTN).astype(bf)
        hs = _out_proj(hs, o_s, ga_s, sma_s, gp_s, gate_s, w_br_a_bf[l], w_out_bf[l], tm=s_new * db,
                       tiles_per_seq=1)
        outs["ks"].append(jnp.transpose(k32_s.reshape(s_new, db, N_HEADS, HEAD_DIM), (1, 0, 2, 3)))
        outs["vs"].append(jnp.transpose(v32_s.reshape(s_new, db, N_HEADS, HEAD_DIM), (1, 0, 2, 3)))
        outs["fs"].append(lf_bt)
        outs["ps"].append(jnp.transpose(tail_s.reshape(POOL_SLOTS, db, D_POOL)[1:], (1, 0, 2)))

    y_p = hp.reshape(b, t, d)
    y_s = jnp.transpose(hs.reshape(s_new, db, d), (1, 0, 2))
    k_p = jnp.transpose(k_stack.reshape(depth, b, N_HEADS, HEAD_DIM, t), (0, 1, 4, 2, 3))
    v_p = jnp.transpose(v_stack.reshape(depth, b, N_HEADS, HEAD_DIM, t), (0, 1, 4, 2, 3))
    return (y_p, y_s, k_p, v_p, jnp.stack(outs["fp"]),
            jnp.stack(outs["pp"]), jnp.stack(outs["ks"]), jnp.stack(outs["vs"]), jnp.stack(outs["fs"]),
            jnp.stack(outs["ps"]))
```

```python
import functools
import math

import jax
import jax.numpy as jnp
from jax import lax
from jax.experimental import pallas as pl
from jax.experimental.pallas import tpu as pltpu

N_HEADS = 8
HEAD_DIM = 64
D_ATTN = N_HEADS * HEAD_DIM
POOL_WINDOWS = (2, 4, 8, 16)
POOL_GC = 128
D_POOL = POOL_GC * len(POOL_WINDOWS)
POOL_HIST = max(POOL_WINDOWS) - 1
POOL_SLOTS = POOL_HIST + 1
EPS = 1e-6
LOG2E = math.log2(math.e)
LANES = 128
F_PAD = LANES
Q_ROWS = 16
VMEM_LIMIT = 56 * 1024 * 1024

C_Q, C_K, C_V, C_GA, C_Z, C_GP = 0, 512, 1024, 1536, 2048, 2560
C_MA = 3072


def _split3(x):
    hi = x.astype(jnp.bfloat16)
    r1 = x - hi.astype(jnp.float32)
    mid = r1.astype(jnp.bfloat16)
    lo = (r1 - mid.astype(jnp.float32)).astype(jnp.bfloat16)
    return hi, mid, lo


def _dot(a, b):
    return jnp.dot(a, b, preferred_element_type=jnp.float32)


def _dot_exact_rhs(sel, x):
    hi, mid, lo = _split3(x)
    return _dot(sel, hi) + _dot(sel, mid) + _dot(sel, lo)


def _dot_exact_lhs(x, sel):
    hi, mid, lo = _split3(x)
    return _dot(hi, sel) + _dot(mid, sel) + _dot(lo, sel)


def _div_pow2(x, n):
    assert n & (n - 1) == 0
    return lax.shift_right_logical(x, n.bit_length() - 1)


def _mod_pow2(x, n):
    assert n & (n - 1) == 0
    return lax.bitwise_and(x, n - 1)


def _silu(x):
    return x * jax.nn.sigmoid(x)


def _log_sigmoid(x):
    return jnp.minimum(x, 0.0) - jnp.log1p(jnp.exp(-jnp.abs(x)))


def _mod_kernel(c_ref, w_ref, b_ref, o_ref):
    a_hi, a_mid, a_lo = _split3(_silu(c_ref[...]))
    w_hi, w_mid, w_lo = _split3(w_ref[...])
    acc = _dot(a_hi, w_hi) + (_dot(a_hi, w_mid) + _dot(a_mid, w_hi))
    acc = acc + (_dot(a_hi, w_lo) + _dot(a_lo, w_hi) + _dot(a_mid, w_mid))
    o_ref[...] = acc + b_ref[...]


def _modulation(c_all, w_ada_bf, b_ada):
    depth, d, d3 = w_ada_bf.shape
    mc = c_all.shape[0]
    tn = 512
    return pl.pallas_call(
        _mod_kernel,
        grid=(depth, d3 // tn),
        in_specs=[pl.BlockSpec((mc, d), lambda l, j: (0, 0)),
                  pl.BlockSpec((None, d, tn), lambda l, j: (l, 0, j)),
                  pl.BlockSpec((None, 1, tn), lambda l, j: (l, 0, j))],
        out_specs=pl.BlockSpec((None, mc, tn), lambda l, j: (l, 0, j)),
        out_shape=jax.ShapeDtypeStruct((depth, mc, d3), jnp.float32),
        name="adaln_mod",
    )(c_all, w_ada_bf, b_ada.reshape(depth, 1, d3))


N_INPROJ_IN = 14


def _cumsum_rows(x):
    n = x.shape[0]
    row = lax.broadcasted_iota(jnp.int32, x.shape, 0)
    step = 1
    while step < n:
        x = x + jnp.where(row >= step, pltpu.roll(x, step, axis=0), 0.0)
        step *= 2
    return x


def _inproj_kernel(*refs, tm, tstride, tiles_per_seq, prompt, n_alias):
    (x_ref, scale_ref, shift_ref, g_ref, wqkv_ref, wrest_ref, wf_ref, bf_ref, qg_ref, kg_ref,
     hist_ref, wpool_ref, pscale_ref, wbrp_ref) = refs[:N_INPROJ_IN]
    (q_ref, k32_ref, kb_ref, v32_ref, vb_ref, logf_ref, ga_ref, sma_ref, gp_ref, tail_ref,
     zext_scr, fcarry_scr) = refs[N_INPROJ_IN + n_alias:]
    hist_rows = POOL_SLOTS * tstride
    i = pl.program_id(0)
    tile_in_seq = lax.rem(i, tiles_per_seq)
    first = tile_in_seq == 0

    @pl.when(first)
    def _():
        zext_scr[0:hist_rows, :] = hist_ref[...]
        fcarry_scr[...] = jnp.zeros_like(fcarry_scr)

    x = x_ref[...]
    ms = jnp.mean(x * x, axis=-1, keepdims=True)
    xn = x * lax.rsqrt(ms + EPS) * g_ref[...]
    u = (xn * (1.0 + scale_ref[...]) + shift_ref[...]).astype(jnp.bfloat16)

    c_f = C_GA + wrest_ref.shape[1]

    def proj(c0, width):
        if c0 < C_GA:
            return _dot(u, wqkv_ref[:, c0:c0 + width])
        if c0 < c_f:
            return _dot(u, wrest_ref[:, c0 - C_GA:c0 - C_GA + width])
        return _dot(u, wf_ref[...])

    lane = lax.broadcasted_iota(jnp.int32, (tm, LANES), 1)
    low_half = lane < HEAD_DIM

    def head_norm(p, gain_ref):
        outs = []
        for j in range(D_ATTN // LANES):
            slab = p[:, j * LANES:(j + 1) * LANES]
            sq = slab * slab
            s_lo = jnp.sum(jnp.where(low_half, sq, 0.0), axis=1, keepdims=True)
            s_hi = jnp.sum(jnp.where(low_half, 0.0, sq), axis=1, keepdims=True)
            r = jnp.where(low_half, lax.rsqrt(s_lo * (1.0 / HEAD_DIM) + EPS), lax.rsqrt(s_hi * (1.0 / HEAD_DIM) + EPS))
            outs.append(slab * r)
        return jnp.concatenate(outs, axis=1) * gain_ref[...]

    logf =_log_sigmoid(proj(c_f, F_PAD) + bf_ref[...])
    logf_ref[...] = logf[:, :N_HEADS]

    qn = head_norm(proj(C_Q, D_ATTN), qg_ref) * (HEAD_DIM ** -0.5 * LOG2E)
    kn = head_norm(proj(C_K, D_ATTN), kg_ref)
    pv = proj(C_V, D_ATTN)
    k32_ref[...] = kn.T if prompt else kn
    v32_ref[...] = pv.T if prompt else pv
    if prompt:
        fc = _cumsum_rows(logf) + fcarry_scr[...]
        fcarry_scr[...] = fc[tm - 1:tm, :]
        pieces = [p.astype(jnp.float32) for p in _split3(fc * LOG2E)]
        for h in range(N_HEADS):
            j = h // 2
            f_hi, f_mid, f_lo = (p[:, h:h + 1] for p in pieces)

            def head_tile(x, h=h, j=j):
                slab = x[:, j * LANES:(j + 1) * LANES]
                return slab if h % 2 == 0 else pltpu.roll(slab, HEAD_DIM, axis=1)

            def tail(a, b, c, rest, first):
                return jnp.where(lane == first, a, jnp.where(lane == first + 1, b,
                                                             jnp.where(lane == first + 2, c, rest)))

            ones_q = jnp.where(lane < HEAD_DIM + 3, 1.0, 0.0)
            ones_k = jnp.where((lane >= HEAD_DIM + 3) & (lane < HEAD_DIM + 6), 1.0, 0.0)
            q_t = jnp.where(low_half, head_tile(qn), tail(f_hi, f_mid, f_lo, ones_q, HEAD_DIM + 3))
            k_t = jnp.where(low_half, head_tile(kn), tail(-f_hi, -f_mid, -f_lo, ones_k, HEAD_DIM))
            v_t = jnp.where(low_half, head_tile(pv), jnp.where(lane == HEAD_DIM, 1.0, 0.0))
            q_ref[:, h * LANES:(h + 1) * LANES] = q_t.astype(jnp.bfloat16)
            kb_ref[:, h * LANES:(h + 1) * LANES] = k_t.astype(jnp.bfloat16)
            vb_ref[:, h * LANES:(h + 1) * LANES] = v_t.astype(jnp.bfloat16)
    else:
        q_ref[...] = qn.astype(jnp.bfloat16)
        kb_ref[...] = kn.astype(jnp.bfloat16)
        vb_ref[...] = pv.astype(jnp.bfloat16)

    ga_ref[...] = _silu(proj(C_GA, D_ATTN)).astype(jnp.bfloat16)

    zext_scr[hist_rows:hist_rows + tm, :] = proj(C_Z, D_POOL)
    if prompt:
        pos = tile_in_seq * tm + lax.broadcasted_iota(jnp.int32, (tm, 1), 0)
    ys = []
    for g, w in enumerate(POOL_WINDOWS):
        col = zext_scr[:, g * POOL_GC:(g + 1) * POOL_GC]
        s = col
        span = 1
        while span < w:
            s = s + pltpu.roll(s, span * tstride, axis=0)
            span *= 2
        if prompt:
            cnt = jnp.minimum(pos + 1, w).astype(jnp.float32)
        else:
            cnt = jnp.float32(w)
        diff = s[hist_rows:, :] / cnt - col[hist_rows:, :]
        ys.append(_dot(diff.astype(jnp.bfloat16), wpool_ref[g]))
    y_pool = jnp.concatenate(ys, axis=1) * pscale_ref[...]
    tail = zext_scr[tm:tm + hist_rows, :]
    tail_ref[...] = tail
    if tiles_per_seq > 1:
        zext_scr[0:hist_rows, :] = tail

    h_p = (y_pool * _silu(proj(C_GP, D_POOL))).astype(jnp.bfloat16)
    d_model = x.shape[1]
    half = d_model // 2
    for c in range(2):
        br_p = _dot(h_p, wbrp_ref[:, c * half:(c + 1) * half])
        m_p = proj(C_MA + d_model + c * half, half)
        gp_ref[:, c * half:(c + 1) * half] = (jax.nn.sigmoid(m_p) * br_p).astype(jnp.bfloat16)
        m_a = proj(C_MA + c * half, half)
        sma_ref[:, c * half:(c + 1) * half] = jax.nn.sigmoid(m_a).astype(jnp.bfloat16)


def _inproj(x, scale, shift, hist, lw, *, layer, tm, tstride, tiles_per_seq, prompt, depth=None, kv_bufs=()):
    rows, d = x.shape
    n_tiles = rows // tm
    hist_rows = POOL_SLOTS * tstride
    bf, f32 = jnp.bfloat16, jnp.float32

    def full(a):
        return pl.BlockSpec(a.shape, lambda i, nd=a.ndim: (0,) * nd)

    def of_layer(a):
        return pl.BlockSpec((None,) + a.shape[1:], lambda i, nd=a.ndim: (layer,) + (0,) * (nd - 1))

    if scale.ndim == 3:
        mod_spec = pl.BlockSpec((None, 1, d), lambda i: (i // tiles_per_seq, 0, 0))
    else:
        mod_spec = pl.BlockSpec((tm, d), lambda i: (i, 0))

    def rowblk(width):
        return pl.BlockSpec((tm, width), lambda i: (i, 0))

    n_seq = n_tiles // tiles_per_seq
    w_att = N_HEADS * LANES if prompt else D_ATTN
    if prompt:
        kv_shape = jax.ShapeDtypeStruct((depth, n_seq, D_ATTN, tiles_per_seq * tm), f32)
        kv_spec = pl.BlockSpec((None, None, D_ATTN, tm),
                               lambda i: (layer, i // tiles_per_seq, 0, lax.rem(i, tiles_per_seq)))
    else:
        kv_shape = jax.ShapeDtypeStruct((rows, D_ATTN), f32)
        kv_spec = rowblk(D_ATTN)
    n_alias = len(kv_bufs)
    out_shape = (
        jax.ShapeDtypeStruct((rows, w_att), bf),
        kv_shape,
        jax.ShapeDtypeStruct((rows, w_att), bf),
        kv_shape,
        jax.ShapeDtypeStruct((rows, w_att), bf),
        jax.ShapeDtypeStruct((rows, N_HEADS), f32),
        jax.ShapeDtypeStruct((rows, D_ATTN), bf),
        jax.ShapeDtypeStruct((rows, d), bf),
        jax.ShapeDtypeStruct((rows, d), bf),
        jax.ShapeDtypeStruct((n_seq, hist_rows, D_POOL), f32),
    )
    out_specs = (rowblk(w_att), kv_spec, rowblk(w_att), kv_spec, rowblk(w_att),
                 rowblk(N_HEADS), rowblk(D_ATTN), rowblk(d), rowblk(d),
                 pl.BlockSpec((None, hist_rows, D_POOL), lambda i: (i // tiles_per_seq, 0, 0)))
    kern = functools.partial(_inproj_kernel, tm=tm, tstride=tstride, tiles_per_seq=tiles_per_seq,
                             prompt=prompt, n_alias=n_alias)
    params = (lw["norm_g"], lw["w_qkv"], lw["w_rest"], lw["w_f"], lw["b_f"], lw["q_gain"], lw["k_gain"])
    params2 = (lw["w_pool"], lw["pool_scale"], lw["w_br_p"])
    operands = (x, scale, shift) + params + (hist,) + params2
    assert len(operands) == N_INPROJ_IN
    return pl.pallas_call(
        kern,
        grid=(n_tiles,),
        in_specs=[rowblk(d), mod_spec, mod_spec] + [of_layer(a) for a in params] + [full(hist)]
                 + [of_layer(a) for a in params2] + [pl.BlockSpec(memory_space=pl.ANY)] * n_alias,
        out_specs=out_specs,
        out_shape=out_shape,
        input_output_aliases={N_INPROJ_IN: 1, N_INPROJ_IN + 1: 3} if n_alias else {},
        scratch_shapes=[pltpu.VMEM((hist_rows + tm, D_POOL), f32), pltpu.VMEM((1, F_PAD), f32)],
        compiler_params=pltpu.CompilerParams(dimension_semantics=("arbitrary",), vmem_limit_bytes=VMEM_LIMIT),
        name="in_proj",
    )(*operands, *kv_bufs)


def _attn_kernel(q_ref, k_ref, v_ref, o_ref, m_scr, acc_scr, s_scr, *, tq, hg):
    qi = pl.program_id(2)
    row = lax.broadcasted_iota(jnp.int32, (tq, tq), 0)
    col = lax.broadcasted_iota(jnp.int32, (tq, tq), 1)
    reps = tq // LANES
    nt = (((1,), (1,)), ((), ()))
    m_scr[...] = jnp.full_like(m_scr, -jnp.inf)
    acc_scr[...] = jnp.zeros_like(acc_scr)

    def scores(kt, hh):
        start = pl.multiple_of(kt * tq, tq)
        sl = slice(hh * LANES, (hh + 1) * LANES)
        return lax.dot_general(q_ref[:, sl], k_ref[pl.ds(start, tq), sl], nt, preferred_element_type=jnp.float32)

    def accumulate(s, kt, hh, masked):
        start = pl.multiple_of(kt * tq, tq)
        sl = slice(hh * LANES, (hh + 1) * LANES)
        if masked:
            s = jnp.where(col <= row, s, -jnp.inf)
        m_prev = m_scr[hh]
        m_next = jnp.maximum(m_prev, jnp.max(s, axis=1, keepdims=True))
        p = jnp.exp2(s - jnp.tile(m_next, (1, reps)))
        alpha = jnp.exp2(m_prev - m_next)
        acc_scr[hh] = alpha * acc_scr[hh] + _dot(p.astype(jnp.bfloat16), v_ref[pl.ds(start, tq), sl])
        m_scr[hh] = m_next

    for hh in range(hg):
        s_scr[hh] = scores(0, hh)

    def body(kt, carry):
        for hh in range(hg):
            s = s_scr[hh]
            s_scr[hh] = scores(kt + 1, hh)
            accumulate(s, kt, hh, False)
        return carry

    lax.fori_loop(0, qi, body, 0)
    for hh in range(hg):
        accumulate(s_scr[hh], qi, hh, True)
    lane = lax.broadcasted_iota(jnp.int32, (tq, LANES), 1)
    for j in range(hg // 2):
        a0, a1 = acc_scr[2 * j], acc_scr[2 * j + 1]
        o0 = a0 / a0[:, HEAD_DIM:HEAD_DIM + 1]
        o1 = a1 / a1[:, HEAD_DIM:HEAD_DIM + 1]
        o_ref[:, j * LANES:(j + 1) * LANES] = jnp.where(lane < HEAD_DIM, o0,
                                                        pltpu.roll(o1, HEAD_DIM, axis=1)).astype(o_ref.dtype)


def _prompt_attention(q, k, v, *, tq, hg):
    b, t, _ = q.shape
    n_t = t // tq
    group = hg * LANES
    kern = functools.partial(_attn_kernel, tq=tq, hg=hg)
    return pl.pallas_call(
        kern,
        grid=(b, N_HEADS // hg, n_t),
        in_specs=[pl.BlockSpec((None, tq, group), lambda bi, hp, qi: (bi, qi, hp)),
                  pl.BlockSpec((None, t, group), lambda bi, hp, qi: (bi, 0, hp)),
                  pl.BlockSpec((None, t, group), lambda bi, hp, qi: (bi, 0, hp))],
        out_specs=pl.BlockSpec((None, tq, hg * HEAD_DIM), lambda bi, hp, qi: (bi, qi, hp)),
        out_shape=jax.ShapeDtypeStruct((b, t, D_ATTN), jnp.bfloat16),
        scratch_shapes=[pltpu.VMEM((hg, tq, LANES), jnp.float32), pltpu.VMEM((hg, tq, LANES), jnp.float32),
                        pltpu.VMEM((hg, tq, tq), jnp.float32)],
        compiler_params=pltpu.CompilerParams(dimension_semantics=("arbitrary", "arbitrary", "arbitrary"),
                                             vmem_limit_bytes=VMEM_LIMIT),
        name="fox_prompt_attn",
    )(q, k, v)


def _sample_attn_kernel(pt_ref, q_ref, kn_ref, vn_ref, lfn_ref, lower_ref, lfc_hbm, kc_hbm, vc_hbm,
                        o_ref, kbuf, vbuf, lbuf, g_scr, sem, *, layer, n_pages, s_new):
    f32, bf = jnp.float32, jnp.bfloat16
    nt = (((1,), (1,)), ((), ()))
    page = lfn_ref.shape[-1]
    b = pl.program_id(0)
    n_b = pl.num_programs(0)
    slot = lax.rem(b, 2)

    def page_copies(seq, slot_):
        out = []
        for p in range(n_pages):
            pg = pt_ref[seq, p]
            lanes = pl.ds(p * page, page)
            out.append(pltpu.make_async_copy(lfc_hbm.at[layer, pg], lbuf.at[slot_, :, lanes], sem.at[0, slot_]))
            out.append(pltpu.make_async_copy(kc_hbm.at[layer, pg], kbuf.at[slot_, :, :, lanes], sem.at[1, slot_]))
            out.append(pltpu.make_async_copy(vc_hbm.at[layer, pg], vbuf.at[slot_, :, :, lanes], sem.at[2, slot_]))
        return out

    @pl.when(b == 0)
    def _():
        for c in page_copies(0, 0):
            c.start()

    @pl.when(b + 1 < n_b)
    def _():
        for c in page_copies(b + 1, 1 - slot):
            c.start()

    for c in page_copies(b, slot):
        c.wait()

    carry = jnp.zeros((N_HEADS, 1), f32)
    for p in reversed(range(n_pages)):
        lf = lbuf[slot, :, p * page:(p + 1) * page]
        g_scr[:, p * page:(p + 1) * page] = (_dot_exact_lhs(lf, lower_ref[...]) + carry) * LOG2E
        carry = carry + jnp.sum(lf, axis=1, keepdims=True)

    n_new = kn_ref.shape[0]
    qi = lax.broadcasted_iota(jnp.int32, (Q_ROWS, n_new), 0)
    ti = lax.broadcasted_iota(jnp.int32, (Q_ROWS, n_new), 1)
    new_ok = (ti <= qi) & (ti < s_new)
    q_row = lax.broadcasted_iota(jnp.int32, (Q_ROWS, 1), 0)
    t_lane = lax.broadcasted_iota(jnp.int32, (1, n_new), 1)
    for h in range(N_HEADS):
        qh = q_ref[h]
        kn_h = kn_ref[:, h * HEAD_DIM:(h + 1) * HEAD_DIM]
        vn_h = vn_ref[:, h * HEAD_DIM:(h + 1) * HEAD_DIM]
        lf_new = lfn_ref[h:h + 1, :]
        c_col = jnp.zeros((Q_ROWS, 1), f32)
        c_row = jnp.zeros((1, n_new), f32)
        for t_i in range(s_new):
            lf_t = lf_new[:, t_i:t_i + 1]
            c_col = c_col + jnp.where(q_row >= t_i, lf_t, 0.0)
            c_row = c_row + jnp.where(t_lane >= t_i, lf_t, 0.0)
        c_col = c_col * LOG2E
        c_row = c_row * LOG2E
        s_n = lax.dot_general(qh, kn_h, nt, preferred_element_type=f32)
        s_n = jnp.where(new_ok, s_n + c_col - c_row, -jnp.inf)
        kt = kbuf[slot, h].astype(bf)
        vt = vbuf[slot, h].astype(bf)
        s = _dot(qh, kt) + c_col + g_scr[h:h + 1, :]
        m = jnp.maximum(jnp.max(s, axis=1, keepdims=True), jnp.max(s_n, axis=1, keepdims=True))
        pr = jnp.exp2(s - m)
        pr_n = jnp.exp2(s_n - m)
        l = jnp.sum(pr, axis=1, keepdims=True) + jnp.sum(pr_n, axis=1, keepdims=True)
        acc = (lax.dot_general(pr.astype(bf), vt, nt, preferred_element_type=f32)
               + _dot(pr_n.astype(bf), vn_h))
        o_ref[h] = acc / l


def _sample_attention(page_table, q16, kn_t, vn_t, lf_new, lower, logf_t, cache_kt, cache_vt, layer, *, s_new):
    db, n_pages = page_table.shape
    heads, hd, page = cache_kt.shape[2:]
    past = n_pages * page
    f32 = jnp.float32

    def per_b(a):
        return pl.BlockSpec((None,) + a.shape[1:], lambda b, pt, nd=a.ndim: (b,) + (0,) * (nd - 1))

    hbm = pl.BlockSpec(memory_space=pl.ANY)
    grid_spec = pltpu.PrefetchScalarGridSpec(
        num_scalar_prefetch=1,
        grid=(db,),
        in_specs=[per_b(q16), per_b(kn_t), per_b(vn_t), per_b(lf_new),
                  pl.BlockSpec(lower.shape, lambda b, pt: (0, 0)), hbm, hbm, hbm],
        out_specs=pl.BlockSpec((None, heads, Q_ROWS, hd), lambda b, pt: (b, 0, 0, 0)),
        scratch_shapes=[pltpu.VMEM((2, heads, hd, past), f32), pltpu.VMEM((2, heads, hd, past), f32),
                        pltpu.VMEM((2, heads, past), f32), pltpu.VMEM((heads, past), f32),
                        pltpu.SemaphoreType.DMA((3, 2))],
    )
    kern = functools.partial(_sample_attn_kernel, layer=layer, n_pages=n_pages, s_new=s_new)
    return pl.pallas_call(
        kern,
        grid_spec=grid_spec,
        out_shape=jax.ShapeDtypeStruct((db, heads, Q_ROWS, hd), f32),
        compiler_params=pltpu.CompilerParams(dimension_semantics=("arbitrary",), vmem_limit_bytes=VMEM_LIMIT),
        name="fox_sample_attn",
    )(page_table, q16, kn_t, vn_t, lf_new, lower, logf_t, cache_kt, cache_vt)


def _out_kernel(x_ref, o_ref, ga_ref, sma_ref, gp_ref, gate_ref, wbra_ref, wout_ref, y_ref):
    h_a = (o_ref[...].astype(jnp.float32) * ga_ref[...].astype(jnp.float32)).astype(jnp.bfloat16)
    br_a = _dot(h_a, wbra_ref[...])
    merged = sma_ref[...].astype(jnp.float32) * br_a + gp_ref[...].astype(jnp.float32)
    y_ref[...] = x_ref[...] + gate_ref[...] * _dot(merged.astype(jnp.bfloat16), wout_ref[...])


def _out_proj(x, o, ga, sma, gp, gate, w_br_a, w_out, *, layer, tm, tiles_per_seq):
    rows, d = x.shape

    def of_layer(a):
        return pl.BlockSpec((None,) + a.shape[1:], lambda i: (layer, 0, 0))

    def rowblk(width):
        return pl.BlockSpec((tm, width), lambda i: (i, 0))

    if gate.ndim == 3:
        gate_spec = pl.BlockSpec((None, 1, d), lambda i: (i // tiles_per_seq, 0, 0))
    else:
        gate_spec = pl.BlockSpec((tm, d), lambda i: (i, 0))
    return pl.pallas_call(
        _out_kernel,
        grid=(rows // tm,),
        in_specs=[rowblk(d), rowblk(D_ATTN), rowblk(D_ATTN), rowblk(d), rowblk(d), gate_spec,
                  of_layer(w_br_a), of_layer(w_out)],
        out_specs=rowblk(d),
        out_shape=jax.ShapeDtypeStruct((rows, d), jnp.float32),
        compiler_params=pltpu.CompilerParams(dimension_semantics=("arbitrary",), vmem_limit_bytes=VMEM_LIMIT),
        name="out_proj",
    )(x, o, ga, sma, gp, gate, w_br_a, w_out)


def kernel(x_prompt, x_sample, cache_k, cache_v, cache_logf, state_pool, page_table, c_prompt, c_sample,
           norm_g, w_ada, b_ada, w_in, b_f, q_norm_g, k_norm_g, w_pool_grp, pool_scale, w_br_a, w_br_p, w_out):
    f32, bf = jnp.float32, jnp.bfloat16
    b, t, d = x_prompt.shape
    db, s_new, _ = x_sample.shape
    depth = w_in.shape[0]
    page = cache_k.shape[2]
    n_pages = page_table.shape[1]
    tm = 512
    tq = 512
    assert t % tm == 0 and t % tq == 0 and db % 8 == 0 and s_new <= Q_ROWS and n_pages == page_table.shape[1]

    o_q, o_k, o_v = 0, D_ATTN, 2 * D_ATTN
    o_f = 3 * D_ATTN
    o_ga = o_f + N_HEADS
    o_z = o_ga + D_ATTN
    o_gp = o_z + D_POOL
    o_m = o_gp + D_POOL
    lw = {
        "w_qkv": w_in[:, :, o_q:o_f].astype(bf),
        "w_rest": w_in[:, :, o_ga:].astype(bf),
        "w_f": jnp.pad(w_in[:, :, o_f:o_ga], ((0, 0), (0, 0), (0, F_PAD - N_HEADS))).astype(bf),
        "b_f": jnp.pad(b_f, ((0, 0), (0, F_PAD - N_HEADS))).reshape(depth, 1, F_PAD),
        "norm_g": norm_g.reshape(depth, 1, d),
        "q_gain": jnp.tile(q_norm_g, (1, N_HEADS)).reshape(depth, 1, D_ATTN),
        "k_gain": jnp.tile(k_norm_g, (1, N_HEADS)).reshape(depth, 1, D_ATTN),
        "w_pool": w_pool_grp.astype(bf),
        "pool_scale": pool_scale.reshape(depth, 1, D_POOL),
        "w_br_p": w_br_p.astype(bf),
    }
    w_br_a_bf, w_out_bf = w_br_a.astype(bf), w_out.astype(bf)

    rp =lax.broadcasted_iota(jnp.int32, (page, page), 0)
    cp = lax.broadcasted_iota(jnp.int32, (page, page), 1)
    lower = (rp > cp).astype(bf)
    cache_kt = jnp.transpose(cache_k, (0, 1, 3, 4, 2))
    cache_vt = jnp.transpose(cache_v, (0, 1, 3, 4, 2))
    logf_t = jnp.transpose(cache_logf, (0, 1, 3, 2))

    mc = db + 8
    c_all = jnp.concatenate([c_sample, c_prompt, jnp.zeros((mc - db - b, d), f32)], axis=0)
    mod = _modulation(c_all, w_ada, b_ada)

    hp = x_prompt.reshape(b * t, d)
    hs = jnp.transpose(x_sample, (1, 0, 2)).reshape(s_new * db, d)
    hist_zero = jnp.zeros((POOL_SLOTS, D_POOL), f32)
    tiles_per_seq = t // tm

    outs = {n: [] for n in ("kp", "vp", "fp", "pp", "ks", "vs", "fs", "ps")}
    for l in range(depth):
        mod_s = mod[l, :db]
        mod_p = mod[l, db:db + b]
        shift_p, scale_p, gate_p = (mod_p[:, j * d:(j + 1) * d].reshape(b, 1, d) for j in range(3))
        mod_s_rows = jnp.tile(mod_s, (s_new, 1))
        shift_s, scale_s, gate_s = (mod_s_rows[:, j * d:(j + 1) * d] for j in range(3))

        (q_p, k_stack, kb_p, v_stack, vb_p, lf_p, ga_p, sma_p, gp_p, tail_p) = _inproj(
            hp, scale_p, shift_p, hist_zero, lw, layer=l, tm=tm, tstride=1, tiles_per_seq=tiles_per_seq,
            prompt=True, depth=depth, kv_bufs=() if l == 0 else (k_stack, v_stack))
        w_aug = N_HEADS * LANES
        o_p = _prompt_attention(q_p.reshape(b, t, w_aug), kb_p.reshape(b, t, w_aug), vb_p.reshape(b, t, w_aug),
                                tq=tq, hg=4)
        hp = _out_proj(hp, o_p.reshape(b * t, D_ATTN), ga_p, sma_p, gp_p, gate_p, w_br_a_bf, w_out_bf,
                       layer=l, tm=tm, tiles_per_seq=tiles_per_seq)
        outs["fp"].append(lf_p.reshape(b, t, N_HEADS))
        outs["pp"].append(tail_p[:, 1:, :])

        hist_s = jnp.concatenate([jnp.zeros((1, db, D_POOL), f32), jnp.transpose(state_pool[l], (1, 0, 2))],
                                 axis=0).reshape(POOL_SLOTS * db, D_POOL)
        (q_s, k32_s, kb_s, v32_s, vb_s, lf_s, ga_s, sma_s, gp_s, tail_s) = _inproj(
            hs, scale_s, shift_s, hist_s, lw, layer=l, tm=s_new * db, tstride=db, tiles_per_seq=1,
            prompt=False)
        q16 = jnp.transpose(q_s.reshape(s_new, db, N_HEADS, HEAD_DIM), (1, 2, 0, 3))
        q16 = jnp.pad(q16, ((0, 0), (0, 0), (0, Q_ROWS - s_new), (0, 0)))
        kn_t = jnp.pad(jnp.transpose(kb_s.reshape(s_new, db, D_ATTN), (1, 0, 2)),
                       ((0, 0), (0, Q_ROWS - s_new), (0, 0)))
        vn_t = jnp.pad(jnp.transpose(vb_s.reshape(s_new, db, D_ATTN), (1, 0, 2)),
                       ((0, 0), (0, Q_ROWS - s_new), (0, 0)))
        lf_bt = jnp.transpose(lf_s.reshape(s_new, db, N_HEADS), (1, 0, 2))
        lf_bh = jnp.transpose(lf_bt, (0, 2, 1))
        lf_new = jnp.pad(lf_bh, ((0, 0), (0, 0), (0, page - s_new)))
        o_s = _sample_attention(page_table, q16, kn_t, vn_t, lf_new, lower, logf_t, cache_kt, cache_vt, l,
                                s_new=s_new)
        o_s = jnp.transpose(o_s[:, :, :s_new, :], (2, 0, 1, 3)).reshape(s_new * db, D_ATTN).astype(bf)
        hs = _out_proj(hs, o_s, ga_s, sma_s, gp_s, gate_s, w_br_a_bf, w_out_bf, layer=l, tm=s_new * db,
                       tiles_per_seq=1)
        outs["ks"].append(jnp.transpose(k32_s.reshape(s_new, db, N_HEADS, HEAD_DIM), (1, 0, 2, 3)))
        outs["vs"].append(jnp.transpose(v32_s.reshape(s_new, db, N_HEADS, HEAD_DIM), (1, 0, 2, 3)))
        outs["fs"].append(lf_bt)
        outs["ps"].append(jnp.transpose(tail_s.reshape(POOL_SLOTS, db, D_POOL)[1:], (1, 0, 2)))

    y_p = hp.reshape(b, t, d)
    y_s = jnp.transpose(hs.reshape(s_new, db, d), (1, 0, 2))
    k_p = jnp.transpose(k_stack.reshape(depth, b, N_HEADS, HEAD_DIM, t), (0, 1, 4, 2, 3))
    v_p = jnp.transpose(v_stack.reshape(depth, b, N_HEADS, HEAD_DIM, t), (0, 1, 4, 2, 3))
    return (y_p, y_s, k_p, v_p, jnp.stack(outs["fp"]),
            jnp.stack(outs["pp"]), jnp.stack(outs["ks"]), jnp.stack(outs["vs"]), jnp.stack(outs["fs"]),
            jnp.stack(outs["ps"]))
```

```python
import functools
import math

import jax
import jax.numpy as jnp
from jax import lax
from jax.experimental import pallas as pl
from jax.experimental.pallas import tpu as pltpu

N_HEADS = 8
HEAD_DIM = 64
D_ATTN = N_HEADS * HEAD_DIM
POOL_WINDOWS = (2, 4, 8, 16)
POOL_GC = 128
D_POOL = POOL_GC * len(POOL_WINDOWS)
POOL_HIST = max(POOL_WINDOWS) - 1
POOL_SLOTS = POOL_HIST + 1
EPS = 1e-6
LOG2E = math.log2(math.e)
LANES = 128
F_PAD = LANES
Q_ROWS = 16
VMEM_LIMIT = 56 * 1024 * 1024

C_Q, C_K, C_V, C_GA, C_Z, C_GP = 0, 512, 1024, 1536, 2048, 2560
C_MA = 3072


def _split3(x):
    hi = x.astype(jnp.bfloat16)
    r1 = x - hi.astype(jnp.float32)
    mid = r1.astype(jnp.bfloat16)
    lo = (r1 - mid.astype(jnp.float32)).astype(jnp.bfloat16)
    return hi, mid, lo


def _dot(a, b):
    return jnp.dot(a, b, preferred_element_type=jnp.float32)


def _dot_exact_rhs(sel, x):
    hi, mid, lo = _split3(x)
    return _dot(sel, hi) + _dot(sel, mid) + _dot(sel, lo)


def _dot_exact_lhs(x, sel):
    hi, mid, lo = _split3(x)
    return _dot(hi, sel) + _dot(mid, sel) + _dot(lo, sel)


def _div_pow2(x, n):
    assert n & (n - 1) == 0
    return lax.shift_right_logical(x, n.bit_length() - 1)


def _mod_pow2(x, n):
    assert n & (n - 1) == 0
    return lax.bitwise_and(x, n - 1)


def _silu(x):
    return x * jax.nn.sigmoid(x)


def _log_sigmoid(x):
    return jnp.minimum(x, 0.0) - jnp.log1p(jnp.exp(-jnp.abs(x)))


def _mod_kernel(c_ref, w_ref, b_ref, o_ref):
    a_hi, a_mid, a_lo = _split3(_silu(c_ref[...]))
    w_hi, w_mid, w_lo = _split3(w_ref[...])
    acc = _dot(a_hi, w_hi) + (_dot(a_hi, w_mid) + _dot(a_mid, w_hi))
    acc = acc + (_dot(a_hi, w_lo) + _dot(a_lo, w_hi) + _dot(a_mid, w_mid))
    o_ref[...] = acc + b_ref[...]


def _modulation(c_all, w_ada_bf, b_ada):
    depth, d, d3 = w_ada_bf.shape
    mc = c_all.shape[0]
    tn = 512
    return pl.pallas_call(
        _mod_kernel,
        grid=(depth, d3 // tn),
        in_specs=[pl.BlockSpec((mc, d), lambda l, j: (0, 0)),
                  pl.BlockSpec((None, d, tn), lambda l, j: (l, 0, j)),
                  pl.BlockSpec((None, 1, tn), lambda l, j: (l, 0, j))],
        out_specs=pl.BlockSpec((None, mc, tn), lambda l, j: (l, 0, j)),
        out_shape=jax.ShapeDtypeStruct((depth, mc, d3), jnp.float32),
        name="adaln_mod",
    )(c_all, w_ada_bf, b_ada.reshape(depth, 1, d3))


N_INPROJ_IN = 14


def _cumsum_rows(x):
    n = x.shape[0]
    row = lax.broadcasted_iota(jnp.int32, x.shape, 0)
    step = 1
    while step < n:
        x = x + jnp.where(row >= step, pltpu.roll(x, step, axis=0), 0.0)
        step *= 2
    return x


def _inproj_kernel(*refs, tm, tstride, tiles_per_seq, prompt, n_alias):
    (x_ref, scale_ref, shift_ref, g_ref, wqkv_ref, wrest_ref, wf_ref, bf_ref, qg_ref, kg_ref,
     hist_ref, wpool_ref, pscale_ref, wbrp_ref) = refs[:N_INPROJ_IN]
    (q_ref, k32_ref, kb_ref, v32_ref, vb_ref, logf_ref, ga_ref, sma_ref, gp_ref, tail_ref,
     zext_scr, fcarry_scr) = refs[N_INPROJ_IN + n_alias:]
    hist_rows = POOL_SLOTS * tstride
    i = pl.program_id(0)
    tile_in_seq = lax.rem(i, tiles_per_seq)
    first = tile_in_seq == 0

    @pl.when(first)
    def _():
        zext_scr[0:hist_rows, :] = hist_ref[...]
        fcarry_scr[...] = jnp.zeros_like(fcarry_scr)

    x = x_ref[...]
    ms = jnp.mean(x * x, axis=-1, keepdims=True)
    xn = x * lax.rsqrt(ms + EPS) * g_ref[...]
    u = (xn * (1.0 + scale_ref[...]) + shift_ref[...]).astype(jnp.bfloat16)

    c_f = C_GA + wrest_ref.shape[1]

    def proj(c0, width):
        if c0 < C_GA:
            return _dot(u, wqkv_ref[:, c0:c0 + width])
        if c0 < c_f:
            return _dot(u, wrest_ref[:, c0 - C_GA:c0 - C_GA + width])
        return _dot(u, wf_ref[...])

    lane = lax.broadcasted_iota(jnp.int32, (tm, LANES), 1)
    low_half = lane < HEAD_DIM

    def head_norm(p, gain_ref):
        outs = []
        for j in range(D_ATTN // LANES):
            slab = p[:, j * LANES:(j + 1) * LANES]
            sq = slab * slab
            s_lo = jnp.sum(jnp.where(low_half, sq, 0.0), axis=1, keepdims=True)
            s_hi = jnp.sum(jnp.where(low_half, 0.0, sq), axis=1, keepdims=True)
            r = jnp.where(low_half, lax.rsqrt(s_lo * (1.0 / HEAD_DIM) + EPS), lax.rsqrt(s_hi * (1.0 / HEAD_DIM) + EPS))
            outs.append(slab * r)
        return jnp.concatenate(outs, axis=1) * gain_ref[...]

    logf =_log_sigmoid(proj(c_f, F_PAD) + bf_ref[...])
    logf_ref[...] = logf[:, :N_HEADS]

    qn = head_norm(proj(C_Q, D_ATTN), qg_ref) * (HEAD_DIM ** -0.5 * LOG2E)
    kn = head_norm(proj(C_K, D_ATTN), kg_ref)
    pv = proj(C_V, D_ATTN)
    k32_ref[...] = kn.T if prompt else kn
    v32_ref[...] = pv.T if prompt else pv
    if prompt:
        fc = _cumsum_rows(logf) + fcarry_scr[...]
        fcarry_scr[...] = fc[tm - 1:tm, :]
        pieces = [p.astype(jnp.float32) for p in _split3(fc * LOG2E)]
        for h in range(N_HEADS):
            j = h // 2
            f_hi, f_mid, f_lo = (p[:, h:h + 1] for p in pieces)

            def head_tile(x, h=h, j=j):
                slab = x[:, j * LANES:(j + 1) * LANES]
                return slab if h % 2 == 0 else pltpu.roll(slab, HEAD_DIM, axis=1)

            def tail(a, b, c, rest, first):
                return jnp.where(lane == first, a, jnp.where(lane == first + 1, b,
                                                             jnp.where(lane == first + 2, c, rest)))

            ones_q = jnp.where(lane < HEAD_DIM + 3, 1.0, 0.0)
            ones_k = jnp.where((lane >= HEAD_DIM + 3) & (lane < HEAD_DIM + 6), 1.0, 0.0)
            q_t = jnp.where(low_half, head_tile(qn), tail(f_hi, f_mid, f_lo, ones_q, HEAD_DIM + 3))
            k_t = jnp.where(low_half, head_tile(kn), tail(-f_hi, -f_mid, -f_lo, ones_k, HEAD_DIM))
            v_t = jnp.where(low_half, head_tile(pv), jnp.where(lane == HEAD_DIM, 1.0, 0.0))
            q_ref[:, h * LANES:(h + 1) * LANES] = q_t.astype(jnp.bfloat16)
            kb_ref[:, h * LANES:(h + 1) * LANES] = k_t.astype(jnp.bfloat16)
            vb_ref[h * LANES:(h + 1) * LANES, :] = v_t.T.astype(jnp.bfloat16)
    else:
        q_ref[...] = qn.astype(jnp.bfloat16)
        kb_ref[...] = kn.astype(jnp.bfloat16)
        vb_ref[...] = pv.astype(jnp.bfloat16)

    ga_ref[...] = _silu(proj(C_GA, D_ATTN)).astype(jnp.bfloat16)

    zext_scr[hist_rows:hist_rows + tm, :] = proj(C_Z, D_POOL)
    if prompt:
        pos = tile_in_seq * tm + lax.broadcasted_iota(jnp.int32, (tm, 1), 0)
    ys = []
    for g, w in enumerate(POOL_WINDOWS):
        col = zext_scr[:, g * POOL_GC:(g + 1) * POOL_GC]
        s = col
        span = 1
        while span < w:
            s = s + pltpu.roll(s, span * tstride, axis=0)
            span *= 2
        if prompt:
            cnt = jnp.minimum(pos + 1, w).astype(jnp.float32)
        else:
            cnt = jnp.float32(w)
        diff = s[hist_rows:, :] / cnt - col[hist_rows:, :]
        ys.append(_dot(diff.astype(jnp.bfloat16), wpool_ref[g]))
    y_pool = jnp.concatenate(ys, axis=1) * pscale_ref[...]
    tail = zext_scr[tm:tm + hist_rows, :]
    tail_ref[...] = tail
    if tiles_per_seq > 1:
        zext_scr[0:hist_rows, :] = tail

    h_p = (y_pool * _silu(proj(C_GP, D_POOL))).astype(jnp.bfloat16)
    d_model = x.shape[1]
    half = d_model // 2
    for c in range(2):
        br_p = _dot(h_p, wbrp_ref[:, c * half:(c + 1) * half])
        m_p = proj(C_MA + d_model + c * half, half)
        gp_ref[:, c * half:(c + 1) * half] = (jax.nn.sigmoid(m_p) * br_p).astype(jnp.bfloat16)
        m_a = proj(C_MA + c * half, half)
        sma_ref[:, c * half:(c + 1) * half] = jax.nn.sigmoid(m_a).astype(jnp.bfloat16)


def _inproj(x, scale, shift, hist, lw, *, layer, tm, tstride, tiles_per_seq, prompt, depth=None, kv_bufs=()):
    rows, d = x.shape
    n_tiles = rows // tm
    hist_rows = POOL_SLOTS * tstride
    bf, f32 = jnp.bfloat16, jnp.float32

    def full(a):
        return pl.BlockSpec(a.shape, lambda i, nd=a.ndim: (0,) * nd)

    def of_layer(a):
        return pl.BlockSpec((None,) + a.shape[1:], lambda i, nd=a.ndim: (layer,) + (0,) * (nd - 1))

    if scale.ndim == 3:
        mod_spec = pl.BlockSpec((None, 1, d), lambda i: (i // tiles_per_seq, 0, 0))
    else:
        mod_spec = pl.BlockSpec((tm, d), lambda i: (i, 0))

    def rowblk(width):
        return pl.BlockSpec((tm, width), lambda i: (i, 0))

    n_seq = n_tiles // tiles_per_seq
    w_att = N_HEADS * LANES if prompt else D_ATTN
    if prompt:
        kv_shape = jax.ShapeDtypeStruct((depth, n_seq, D_ATTN, tiles_per_seq * tm), f32)
        kv_spec = pl.BlockSpec((None, None, D_ATTN, tm),
                               lambda i: (layer, i // tiles_per_seq, 0, lax.rem(i, tiles_per_seq)))
        vb_shape = jax.ShapeDtypeStruct((n_seq, w_att, tiles_per_seq * tm), bf)
        vb_spec = pl.BlockSpec((None, w_att, tm), lambda i: (i // tiles_per_seq, 0, lax.rem(i, tiles_per_seq)))
    else:
        kv_shape = jax.ShapeDtypeStruct((rows, D_ATTN), f32)
        kv_spec = rowblk(D_ATTN)
        vb_shape = jax.ShapeDtypeStruct((rows, w_att), bf)
        vb_spec = rowblk(w_att)
    n_alias = len(kv_bufs)
    out_shape = (
        jax.ShapeDtypeStruct((rows, w_att), bf),
        kv_shape,
        jax.ShapeDtypeStruct((rows, w_att), bf),
        kv_shape,
        vb_shape,
        jax.ShapeDtypeStruct((rows, N_HEADS), f32),
        jax.ShapeDtypeStruct((rows, D_ATTN), bf),
        jax.ShapeDtypeStruct((rows, d), bf),
        jax.ShapeDtypeStruct((rows, d), bf),
        jax.ShapeDtypeStruct((n_seq, hist_rows, D_POOL), f32),
    )
    out_specs = (rowblk(w_att), kv_spec, rowblk(w_att), kv_spec, vb_spec,
                 rowblk(N_HEADS), rowblk(D_ATTN), rowblk(d), rowblk(d),
                 pl.BlockSpec((None, hist_rows, D_POOL), lambda i: (i // tiles_per_seq, 0, 0)))
    kern = functools.partial(_inproj_kernel, tm=tm, tstride=tstride, tiles_per_seq=tiles_per_seq,
                             prompt=prompt, n_alias=n_alias)
    params = (lw["norm_g"], lw["w_qkv"], lw["w_rest"], lw["w_f"], lw["b_f"], lw["q_gain"], lw["k_gain"])
    params2 = (lw["w_pool"], lw["pool_scale"], lw["w_br_p"])
    operands = (x, scale, shift) + params + (hist,) + params2
    assert len(operands) == N_INPROJ_IN
    return pl.pallas_call(
        kern,
        grid=(n_tiles,),
        in_specs=[rowblk(d), mod_spec, mod_spec] + [of_layer(a) for a in params] + [full(hist)]
                 + [of_layer(a) for a in params2] + [pl.BlockSpec(memory_space=pl.ANY)] * n_alias,
        out_specs=out_specs,
        out_shape=out_shape,
        input_output_aliases={N_INPROJ_IN: 1, N_INPROJ_IN + 1: 3} if n_alias else {},
        scratch_shapes=[pltpu.VMEM((hist_rows + tm, D_POOL), f32), pltpu.VMEM((1, F_PAD), f32)],
        compiler_params=pltpu.CompilerParams(dimension_semantics=("arbitrary",), vmem_limit_bytes=VMEM_LIMIT),
        name="in_proj",
    )(*operands, *kv_bufs)


def _attn_kernel(q_ref, k_ref, v_ref, o_ref, m_scr, acc_scr, s_scr, *, tq, hg):
    qi = pl.program_id(2)
    row = lax.broadcasted_iota(jnp.int32, (tq, tq), 0)
    col = lax.broadcasted_iota(jnp.int32, (tq, tq), 1)
    nt = (((1,), (1,)), ((), ()))
    m_scr[...] = jnp.full_like(m_scr, -jnp.inf)
    acc_scr[...] = jnp.zeros_like(acc_scr)

    def scores(kt, hh):
        start = pl.multiple_of(kt * tq, tq)
        sl = slice(hh * LANES, (hh + 1) * LANES)
        return lax.dot_general(k_ref[pl.ds(start, tq), sl], q_ref[:, sl], nt, preferred_element_type=jnp.float32)

    def accumulate(s, kt, hh, masked):
        start = pl.multiple_of(kt * tq, tq)
        sl = slice(hh * LANES, (hh + 1) * LANES)
        if masked:
            s = jnp.where(row <= col, s, -jnp.inf)
        m_prev = m_scr[hh]
        m_next = jnp.maximum(m_prev, jnp.max(s, axis=0, keepdims=True))
        p = jnp.exp2(s - m_next)
        alpha = jnp.exp2(m_prev - m_next)
        acc_scr[hh] = alpha * acc_scr[hh] + _dot(v_ref[sl, pl.ds(start, tq)], p.astype(jnp.bfloat16))
        m_scr[hh] = m_next

    for hh in range(hg):
        s_scr[hh] = scores(0, hh)

    def body(kt, carry):
        for hh in range(hg):
            s = s_scr[hh]
            s_scr[hh] = scores(kt + 1, hh)
            accumulate(s, kt, hh, False)
        return carry

    lax.fori_loop(0, qi, body, 0)
    for hh in range(hg):
        accumulate(s_scr[hh], qi, hh, True)
    lane = lax.broadcasted_iota(jnp.int32, (tq, LANES), 1)
    for j in range(hg // 2):
        a0, a1 = acc_scr[2 * j], acc_scr[2 * j + 1]
        o0 = (a0 / a0[HEAD_DIM:HEAD_DIM + 1, :]).T
        o1 = (a1 / a1[HEAD_DIM:HEAD_DIM + 1, :]).T
        o_ref[:, j * LANES:(j + 1) * LANES] = jnp.where(lane < HEAD_DIM, o0,
                                                        pltpu.roll(o1, HEAD_DIM, axis=1)).astype(o_ref.dtype)


def _prompt_attention(q, k, v, *, tq, hg):
    b, t, _ = q.shape
    n_t = t // tq
    group = hg * LANES
    kern = functools.partial(_attn_kernel, tq=tq, hg=hg)
    return pl.pallas_call(
        kern,
        grid=(b, N_HEADS // hg, n_t),
        in_specs=[pl.BlockSpec((None, tq, group), lambda bi, hp, qi: (bi, qi, hp)),
                  pl.BlockSpec((None, t, group), lambda bi, hp, qi: (bi, 0, hp)),
                  pl.BlockSpec((None, group, t), lambda bi, hp, qi: (bi, hp, 0))],
        out_specs=pl.BlockSpec((None, tq, hg * HEAD_DIM), lambda bi, hp, qi: (bi, qi, hp)),
        out_shape=jax.ShapeDtypeStruct((b, t, D_ATTN), jnp.bfloat16),
        scratch_shapes=[pltpu.VMEM((hg, 1, tq), jnp.float32), pltpu.VMEM((hg, LANES, tq), jnp.float32),
                        pltpu.VMEM((hg, tq, tq), jnp.float32)],
        compiler_params=pltpu.CompilerParams(dimension_semantics=("arbitrary", "arbitrary", "arbitrary"),
                                             vmem_limit_bytes=VMEM_LIMIT),
        name="fox_prompt_attn",
    )(q, k, v)


def _sample_attn_kernel(pt_ref, q_ref, kn_ref, vn_ref, lfn_ref, lower_ref, lfc_hbm, kc_hbm, vc_hbm,
                        o_ref, kbuf, vbuf, lbuf, g_scr, sem, *, layer, n_pages, s_new):
    f32, bf = jnp.float32, jnp.bfloat16
    nt = (((1,), (1,)), ((), ()))
    page = lfn_ref.shape[-1]
    b = pl.program_id(0)
    n_b = pl.num_programs(0)
    slot = lax.rem(b, 2)

    def page_copies(seq, slot_):
        out = []
        for p in range(n_pages):
            pg = pt_ref[seq, p]
            lanes = pl.ds(p * page, page)
            out.append(pltpu.make_async_copy(lfc_hbm.at[layer, pg], lbuf.at[slot_, :, lanes], sem.at[0, slot_]))
            out.append(pltpu.make_async_copy(kc_hbm.at[layer, pg], kbuf.at[slot_, :, :, lanes], sem.at[1, slot_]))
            out.append(pltpu.make_async_copy(vc_hbm.at[layer, pg], vbuf.at[slot_, :, :, lanes], sem.at[2, slot_]))
        return out

    @pl.when(b == 0)
    def _():
        for c in page_copies(0, 0):
            c.start()

    @pl.when(b + 1 < n_b)
    def _():
        for c in page_copies(b + 1, 1 - slot):
            c.start()

    for c in page_copies(b, slot):
        c.wait()

    carry = jnp.zeros((N_HEADS, 1), f32)
    for p in reversed(range(n_pages)):
        lf = lbuf[slot, :, p * page:(p + 1) * page]
        g_scr[:, p * page:(p + 1) * page] = (_dot_exact_lhs(lf, lower_ref[...]) + carry) * LOG2E
        carry = carry + jnp.sum(lf, axis=1, keepdims=True)

    n_new = kn_ref.shape[0]
    qi = lax.broadcasted_iota(jnp.int32, (Q_ROWS, n_new), 0)
    ti = lax.broadcasted_iota(jnp.int32, (Q_ROWS, n_new), 1)
    new_ok = (ti <= qi) & (ti < s_new)
    q_row = lax.broadcasted_iota(jnp.int32, (Q_ROWS, 1), 0)
    t_lane = lax.broadcasted_iota(jnp.int32, (1, n_new), 1)
    for h in range(N_HEADS):
        qh = q_ref[h]
        kn_h = kn_ref[:, h * HEAD_DIM:(h + 1) * HEAD_DIM]
        vn_h = vn_ref[:, h * HEAD_DIM:(h + 1) * HEAD_DIM]
        lf_new = lfn_ref[h:h + 1, :]
        c_col = jnp.zeros((Q_ROWS, 1), f32)
        c_row = jnp.zeros((1, n_new), f32)
        for t_i in range(s_new):
            lf_t = lf_new[:, t_i:t_i + 1]
            c_col = c_col + jnp.where(q_row >= t_i, lf_t, 0.0)
            c_row = c_row + jnp.where(t_lane >= t_i, lf_t, 0.0)
        c_col = c_col * LOG2E
        c_row = c_row * LOG2E
        s_n = lax.dot_general(qh, kn_h, nt, preferred_element_type=f32)
        s_n = jnp.where(new_ok, s_n + c_col - c_row, -jnp.inf)
        kt = kbuf[slot, h].astype(bf)
        vt = vbuf[slot, h].astype(bf)
        s = _dot(qh, kt) + c_col + g_scr[h:h + 1, :]
        m = jnp.maximum(jnp.max(s, axis=1, keepdims=True), jnp.max(s_n, axis=1, keepdims=True))
        pr = jnp.exp2(s - m)
        pr_n = jnp.exp2(s_n - m)
        l = jnp.sum(pr, axis=1, keepdims=True) + jnp.sum(pr_n, axis=1, keepdims=True)
        acc = (lax.dot_general(pr.astype(bf), vt, nt, preferred_element_type=f32)
               + _dot(pr_n.astype(bf), vn_h))
        o_ref[h] = acc / l


def _sample_attention(page_table, q16, kn_t, vn_t, lf_new, lower, logf_t, cache_kt, cache_vt, layer, *, s_new):
    db, n_pages = page_table.shape
    heads, hd, page = cache_kt.shape[2:]
    past = n_pages * page
    f32 = jnp.float32

    def per_b(a):
        return pl.BlockSpec((None,) + a.shape[1:], lambda b, pt, nd=a.ndim: (b,) + (0,) * (nd - 1))

    hbm = pl.BlockSpec(memory_space=pl.ANY)
    grid_spec = pltpu.PrefetchScalarGridSpec(
        num_scalar_prefetch=1,
        grid=(db,),
        in_specs=[per_b(q16), per_b(kn_t), per_b(vn_t), per_b(lf_new),
                  pl.BlockSpec(lower.shape, lambda b, pt: (0, 0)), hbm, hbm, hbm],
        out_specs=pl.BlockSpec((None, heads, Q_ROWS, hd), lambda b, pt: (b, 0, 0, 0)),
        scratch_shapes=[pltpu.VMEM((2, heads, hd, past), f32), pltpu.VMEM((2, heads, hd, past), f32),
                        pltpu.VMEM((2, heads, past), f32), pltpu.VMEM((heads, past), f32),
                        pltpu.SemaphoreType.DMA((3, 2))],
    )
    kern = functools.partial(_sample_attn_kernel, layer=layer, n_pages=n_pages, s_new=s_new)
    return pl.pallas_call(
        kern,
        grid_spec=grid_spec,
        out_shape=jax.ShapeDtypeStruct((db, heads, Q_ROWS, hd), f32),
        compiler_params=pltpu.CompilerParams(dimension_semantics=("arbitrary",), vmem_limit_bytes=VMEM_LIMIT),
        name="fox_sample_attn",
    )(page_table, q16, kn_t, vn_t, lf_new, lower, logf_t, cache_kt, cache_vt)


def _out_kernel(x_ref, o_ref, ga_ref, sma_ref, gp_ref, gate_ref, wbra_ref, wout_ref, y_ref):
    h_a = (o_ref[...].astype(jnp.float32) * ga_ref[...].astype(jnp.float32)).astype(jnp.bfloat16)
    br_a = _dot(h_a, wbra_ref[...])
    merged = sma_ref[...].astype(jnp.float32) * br_a + gp_ref[...].astype(jnp.float32)
    y_ref[...] = x_ref[...] + gate_ref[...] * _dot(merged.astype(jnp.bfloat16), wout_ref[...])


def _out_proj(x, o, ga, sma, gp, gate, w_br_a, w_out, *, layer, tm, tiles_per_seq):
    rows, d = x.shape

    def of_layer(a):
        return pl.BlockSpec((None,) + a.shape[1:], lambda i: (layer, 0, 0))

    def rowblk(width):
        return pl.BlockSpec((tm, width), lambda i: (i, 0))

    if gate.ndim == 3:
        gate_spec = pl.BlockSpec((None, 1, d), lambda i: (i // tiles_per_seq, 0, 0))
    else:
        gate_spec = pl.BlockSpec((tm, d), lambda i: (i, 0))
    return pl.pallas_call(
        _out_kernel,
        grid=(rows // tm,),
        in_specs=[rowblk(d), rowblk(D_ATTN), rowblk(D_ATTN), rowblk(d), rowblk(d), gate_spec,
                  of_layer(w_br_a), of_layer(w_out)],
        out_specs=rowblk(d),
        out_shape=jax.ShapeDtypeStruct((rows, d), jnp.float32),
        compiler_params=pltpu.CompilerParams(dimension_semantics=("arbitrary",), vmem_limit_bytes=VMEM_LIMIT),
        name="out_proj",
    )(x, o, ga, sma, gp, gate, w_br_a, w_out)


def kernel(x_prompt, x_sample, cache_k, cache_v, cache_logf, state_pool, page_table, c_prompt, c_sample,
           norm_g, w_ada, b_ada, w_in, b_f, q_norm_g, k_norm_g, w_pool_grp, pool_scale, w_br_a, w_br_p, w_out):
    f32, bf = jnp.float32, jnp.bfloat16
    b, t, d = x_prompt.shape
    db, s_new, _ = x_sample.shape
    depth = w_in.shape[0]
    page = cache_k.shape[2]
    n_pages = page_table.shape[1]
    tm = 512
    tq = 512
    assert t % tm == 0 and t % tq == 0 and db % 8 == 0 and s_new <= Q_ROWS and n_pages == page_table.shape[1]

    o_q, o_k, o_v = 0, D_ATTN, 2 * D_ATTN
    o_f = 3 * D_ATTN
    o_ga = o_f + N_HEADS
    o_z = o_ga + D_ATTN
    o_gp = o_z + D_POOL
    o_m = o_gp + D_POOL
    lw = {
        "w_qkv": w_in[:, :, o_q:o_f].astype(bf),
        "w_rest": w_in[:, :, o_ga:].astype(bf),
        "w_f": jnp.pad(w_in[:, :, o_f:o_ga], ((0, 0), (0, 0), (0, F_PAD - N_HEADS))).astype(bf),
        "b_f": jnp.pad(b_f, ((0, 0), (0, F_PAD - N_HEADS))).reshape(depth, 1, F_PAD),
        "norm_g": norm_g.reshape(depth, 1, d),
        "q_gain": jnp.tile(q_norm_g, (1, N_HEADS)).reshape(depth, 1, D_ATTN),
        "k_gain": jnp.tile(k_norm_g, (1, N_HEADS)).reshape(depth, 1, D_ATTN),
        "w_pool": w_pool_grp.astype(bf),
        "pool_scale": pool_scale.reshape(depth, 1, D_POOL),
        "w_br_p": w_br_p.astype(bf),
    }
    w_br_a_bf, w_out_bf = w_br_a.astype(bf), w_out.astype(bf)

    rp =lax.broadcasted_iota(jnp.int32, (page, page), 0)
    cp = lax.broadcasted_iota(jnp.int32, (page, page), 1)
    lower = (rp > cp).astype(bf)
    cache_kt = jnp.transpose(cache_k, (0, 1, 3, 4, 2))
    cache_vt = jnp.transpose(cache_v, (0, 1, 3, 4, 2))
    logf_t = jnp.transpose(cache_logf, (0, 1, 3, 2))

    mc = db + 8
    c_all = jnp.concatenate([c_sample, c_prompt, jnp.zeros((mc - db - b, d), f32)], axis=0)
    mod = _modulation(c_all, w_ada, b_ada)

    hp = x_prompt.reshape(b * t, d)
    hs = jnp.transpose(x_sample, (1, 0, 2)).reshape(s_new * db, d)
    hist_zero = jnp.zeros((POOL_SLOTS, D_POOL), f32)
    tiles_per_seq = t // tm

    outs = {n: [] for n in ("kp", "vp", "fp", "pp", "ks", "vs", "fs", "ps")}
    for l in range(depth):
        mod_s = mod[l, :db]
        mod_p = mod[l, db:db + b]
        shift_p, scale_p, gate_p = (mod_p[:, j * d:(j + 1) * d].reshape(b, 1, d) for j in range(3))
        mod_s_rows = jnp.tile(mod_s, (s_new, 1))
        shift_s, scale_s, gate_s = (mod_s_rows[:, j * d:(j + 1) * d] for j in range(3))

        (q_p, k_stack, kb_p, v_stack, vb_p, lf_p, ga_p, sma_p, gp_p, tail_p) = _inproj(
            hp, scale_p, shift_p, hist_zero, lw, layer=l, tm=tm, tstride=1, tiles_per_seq=tiles_per_seq,
            prompt=True, depth=depth, kv_bufs=() if l == 0 else (k_stack, v_stack))
        w_aug = N_HEADS * LANES
        o_p = _prompt_attention(q_p.reshape(b, t, w_aug), kb_p.reshape(b, t, w_aug), vb_p, tq=tq, hg=4)
        hp = _out_proj(hp, o_p.reshape(b * t, D_ATTN), ga_p, sma_p, gp_p, gate_p, w_br_a_bf, w_out_bf,
                       layer=l, tm=tm, tiles_per_seq=tiles_per_seq)
        outs["fp"].append(lf_p.reshape(b, t, N_HEADS))
        outs["pp"].append(tail_p[:, 1:, :])

        hist_s = jnp.concatenate([jnp.zeros((1, db, D_POOL), f32), jnp.transpose(state_pool[l], (1, 0, 2))],
                                 axis=0).reshape(POOL_SLOTS * db, D_POOL)
        (q_s, k32_s, kb_s, v32_s, vb_s, lf_s, ga_s, sma_s, gp_s, tail_s) = _inproj(
            hs, scale_s, shift_s, hist_s, lw, layer=l, tm=s_new * db, tstride=db, tiles_per_seq=1,
            prompt=False)
        q16 = jnp.transpose(q_s.reshape(s_new, db, N_HEADS, HEAD_DIM), (1, 2, 0, 3))
        q16 = jnp.pad(q16, ((0, 0), (0, 0), (0, Q_ROWS - s_new), (0, 0)))
        kn_t = jnp.pad(jnp.transpose(kb_s.reshape(s_new, db, D_ATTN), (1, 0, 2)),
                       ((0, 0), (0, Q_ROWS - s_new), (0, 0)))
        vn_t = jnp.pad(jnp.transpose(vb_s.reshape(s_new, db, D_ATTN), (1, 0, 2)),
                       ((0, 0), (0, Q_ROWS - s_new), (0, 0)))
        lf_bt = jnp.transpose(lf_s.reshape(s_new, db, N_HEADS), (1, 0, 2))
        lf_bh = jnp.transpose(lf_bt, (0, 2, 1))
        lf_new = jnp.pad(lf_bh, ((0, 0), (0, 0), (0, page - s_new)))
        o_s = _sample_attention(page_table, q16, kn_t, vn_t, lf_new, lower, logf_t, cache_kt, cache_vt, l,
                                s_new=s_new)
        o_s = jnp.transpose(o_s[:, :, :s_new, :], (2, 0, 1, 3)).reshape(s_new * db, D_ATTN).astype(bf)
        hs = _out_proj(hs, o_s, ga_s, sma_s, gp_s, gate_s, w_br_a_bf, w_out_bf, layer=l, tm=s_new * db,
                       tiles_per_seq=1)
        outs["ks"].append(jnp.transpose(k32_s.reshape(s_new, db, N_HEADS, HEAD_DIM), (1, 0, 2, 3)))
        outs["vs"].append(jnp.transpose(v32_s.reshape(s_new, db, N_HEADS, HEAD_DIM), (1, 0, 2, 3)))
        outs["fs"].append(lf_bt)
        outs["ps"].append(jnp.transpose(tail_s.reshape(POOL_SLOTS, db, D_POOL)[1:], (1, 0, 2)))

    y_p = hp.reshape(b, t, d)
    y_s = jnp.transpose(hs.reshape(s_new, db, d), (1, 0, 2))
    k_p = jnp.transpose(k_stack.reshape(depth, b, N_HEADS, HEAD_DIM, t), (0, 1, 4, 2, 3))
    v_p = jnp.transpose(v_stack.reshape(depth, b, N_HEADS, HEAD_DIM, t), (0, 1, 4, 2, 3))
    return (y_p, y_s, k_p, v_p, jnp.stack(outs["fp"]),
            jnp.stack(outs["pp"]), jnp.stack(outs["ks"]), jnp.stack(outs["vs"]), jnp.stack(outs["fs"]),
            jnp.stack(outs["ps"]))
```

```python
import functools
import math

import jax
import jax.numpy as jnp
from jax import lax
from jax.experimental import pallas as pl
from jax.experimental.pallas import tpu as pltpu

N_HEADS = 8
HEAD_DIM = 64
D_ATTN = N_HEADS * HEAD_DIM
POOL_WINDOWS = (2, 4, 8, 16)
POOL_GC = 128
D_POOL = POOL_GC * len(POOL_WINDOWS)
POOL_HIST = max(POOL_WINDOWS) - 1
POOL_SLOTS = POOL_HIST + 1
EPS = 1e-6
LOG2E = math.log2(math.e)
LANES = 128
F_PAD = LANES
Q_ROWS = 16
VMEM_LIMIT = 56 * 1024 * 1024

C_Q, C_K, C_V, C_GA, C_Z, C_GP = 0, 512, 1024, 1536, 2048, 2560
C_MA = 3072


def _split3(x):
    hi = x.astype(jnp.bfloat16)
    r1 = x - hi.astype(jnp.float32)
    mid = r1.astype(jnp.bfloat16)
    lo = (r1 - mid.astype(jnp.float32)).astype(jnp.bfloat16)
    return hi, mid, lo


def _dot(a, b):
    return jnp.dot(a, b, preferred_element_type=jnp.float32)


def _dot_exact_lhs(x, sel):
    hi, mid, lo = _split3(x)
    return _dot(hi, sel) + _dot(mid, sel) + _dot(lo, sel)


def _silu(x):
    return x * jax.nn.sigmoid(x)


def _log_sigmoid(x):
    return jnp.minimum(x, 0.0) - jnp.log1p(jnp.exp(-jnp.abs(x)))


def _mod_kernel(c_ref, w_ref, b_ref, o_ref):
    a_hi, a_mid, a_lo = _split3(_silu(c_ref[...]))
    w_hi, w_mid, w_lo = _split3(w_ref[...])
    acc = _dot(a_hi, w_hi) + (_dot(a_hi, w_mid) + _dot(a_mid, w_hi))
    acc = acc + (_dot(a_hi, w_lo) + _dot(a_lo, w_hi) + _dot(a_mid, w_mid))
    o_ref[...] = acc + b_ref[...]


def _modulation(c_all, w_ada_bf, b_ada):
    depth, d, d3 = w_ada_bf.shape
    mc = c_all.shape[0]
    tn = 512
    return pl.pallas_call(
        _mod_kernel,
        grid=(depth, d3 // tn),
        in_specs=[pl.BlockSpec((mc, d), lambda l, j: (0, 0)),
                  pl.BlockSpec((None, d, tn), lambda l, j: (l, 0, j)),
                  pl.BlockSpec((None, 1, tn), lambda l, j: (l, 0, j))],
        out_specs=pl.BlockSpec((None, mc, tn), lambda l, j: (l, 0, j)),
        out_shape=jax.ShapeDtypeStruct((depth, mc, d3), jnp.float32),
        name="adaln_mod",
    )(c_all, w_ada_bf, b_ada.reshape(depth, 1, d3))


N_INPROJ_IN = 14


def _cumsum_rows(x):
    n = x.shape[0]
    row = lax.broadcasted_iota(jnp.int32, x.shape, 0)
    step = 1
    while step < n:
        x = x + jnp.where(row >= step, pltpu.roll(x, step, axis=0), 0.0)
        step *= 2
    return x


def _inproj_kernel(*refs, tm, tstride, tiles_per_seq, prompt, n_alias):
    (x_ref, scale_ref, shift_ref, g_ref, wqkv_ref, wrest_ref, wf_ref, bf_ref, qg_ref, kg_ref,
     hist_ref, wpool_ref, pscale_ref, wbrp_ref) = refs[:N_INPROJ_IN]
    (q_ref, k32_ref, kb_ref, v32_ref, vb_ref, logf_ref, ga_ref, sma_ref, gp_ref, tail_ref,
     zext_scr, fcarry_scr) = refs[N_INPROJ_IN + n_alias:]
    hist_rows = POOL_SLOTS * tstride
    i = pl.program_id(0)
    tile_in_seq = lax.rem(i, tiles_per_seq)
    first = tile_in_seq == 0

    @pl.when(first)
    def _():
        zext_scr[0:hist_rows, :] = hist_ref[...]
        fcarry_scr[...] = jnp.zeros_like(fcarry_scr)

    x = x_ref[...]
    ms = jnp.mean(x * x, axis=-1, keepdims=True)
    xn = x * lax.rsqrt(ms + EPS) * g_ref[...]
    u = (xn * (1.0 + scale_ref[...]) + shift_ref[...]).astype(jnp.bfloat16)

    c_f = C_GA + wrest_ref.shape[1]

    def proj(c0, width):
        if c0 < C_GA:
            return _dot(u, wqkv_ref[:, c0:c0 + width])
        if c0 < c_f:
            return _dot(u, wrest_ref[:, c0 - C_GA:c0 - C_GA + width])
        return _dot(u, wf_ref[...])

    lane = lax.broadcasted_iota(jnp.int32, (tm, LANES), 1)
    low_half = lane < HEAD_DIM

    def head_norm(p, gain_ref):
        outs = []
        for j in range(D_ATTN // LANES):
            slab = p[:, j * LANES:(j + 1) * LANES]
            sq = slab * slab
            s_lo = jnp.sum(jnp.where(low_half, sq, 0.0), axis=1, keepdims=True)
            s_hi = jnp.sum(jnp.where(low_half, 0.0, sq), axis=1, keepdims=True)
            r = jnp.where(low_half, lax.rsqrt(s_lo * (1.0 / HEAD_DIM) + EPS), lax.rsqrt(s_hi * (1.0 / HEAD_DIM) + EPS))
            outs.append(slab * r)
        return jnp.concatenate(outs, axis=1) * gain_ref[...]

    logf =_log_sigmoid(proj(c_f, F_PAD) + bf_ref[...])
    logf_ref[...] = logf[:, :N_HEADS]

    qn = head_norm(proj(C_Q, D_ATTN), qg_ref) * (HEAD_DIM ** -0.5 * LOG2E)
    kn = head_norm(proj(C_K, D_ATTN), kg_ref)
    pv = proj(C_V, D_ATTN)
    if prompt:
        pv_t = pv.T
        k32_ref[...] = kn.T
        v32_ref[...] = pv_t
        fc = _cumsum_rows(logf) + fcarry_scr[...]
        fcarry_scr[...] = fc[tm - 1:tm, :]
        ones_row = jnp.where(lax.broadcasted_iota(jnp.int32, (HEAD_DIM, tm), 0) == 0, 1.0, 0.0)
        pieces = [p.astype(jnp.float32) for p in _split3(fc * LOG2E)]
        for h in range(N_HEADS):
            j = h // 2
            f_hi, f_mid, f_lo = (p[:, h:h + 1] for p in pieces)

            def head_tile(x, h=h, j=j):
                slab = x[:, j * LANES:(j + 1) * LANES]
                return slab if h % 2 == 0 else pltpu.roll(slab, HEAD_DIM, axis=1)

            def tail(a, b, c, rest, first):
                return jnp.where(lane == first, a, jnp.where(lane == first + 1, b,
                                                             jnp.where(lane == first + 2, c, rest)))

            ones_q = jnp.where(lane < HEAD_DIM + 3, 1.0, 0.0)
            ones_k = jnp.where((lane >= HEAD_DIM + 3) & (lane < HEAD_DIM + 6), 1.0, 0.0)
            q_t = jnp.where(low_half, head_tile(qn), tail(f_hi, f_mid, f_lo, ones_q, HEAD_DIM + 3))
            k_t = jnp.where(low_half, head_tile(kn), tail(-f_hi, -f_mid, -f_lo, ones_k, HEAD_DIM))
            q_ref[:, h * LANES:(h + 1) * LANES] = q_t.astype(jnp.bfloat16)
            kb_ref[:, h * LANES:(h + 1) * LANES] = k_t.astype(jnp.bfloat16)
            vb_ref[h * LANES:h * LANES + HEAD_DIM, :] = pv_t[h * HEAD_DIM:(h + 1) * HEAD_DIM, :].astype(jnp.bfloat16)
            vb_ref[h * LANES + HEAD_DIM:(h + 1) * LANES, :] = ones_row.astype(jnp.bfloat16)
    else:
        k32_ref[...] = kn
        v32_ref[...] = pv
        q_ref[...] = qn.astype(jnp.bfloat16)
        kb_ref[...] = kn.astype(jnp.bfloat16)
        vb_ref[...] = pv.astype(jnp.bfloat16)

    ga_ref[...] = _silu(proj(C_GA, D_ATTN)).astype(jnp.bfloat16)

    zext_scr[hist_rows:hist_rows + tm, :] = proj(C_Z, D_POOL)
    if prompt:
        pos = tile_in_seq * tm + lax.broadcasted_iota(jnp.int32, (tm, 1), 0)
    ys = []
    for g, w in enumerate(POOL_WINDOWS):
        col = zext_scr[:, g * POOL_GC:(g + 1) * POOL_GC]
        s = col
        span = 1
        while span < w:
            s = s + pltpu.roll(s, span * tstride, axis=0)
            span *= 2
        if prompt:
            cnt = jnp.minimum(pos + 1, w).astype(jnp.float32)
        else:
            cnt = jnp.float32(w)
        diff = s[hist_rows:, :] / cnt - col[hist_rows:, :]
        ys.append(_dot(diff.astype(jnp.bfloat16), wpool_ref[g]))
    y_pool = jnp.concatenate(ys, axis=1) * pscale_ref[...]
    tail = zext_scr[tm:tm + hist_rows, :]
    tail_ref[...] = tail
    if tiles_per_seq > 1:
        zext_scr[0:hist_rows, :] = tail

    h_p = (y_pool * _silu(proj(C_GP, D_POOL))).astype(jnp.bfloat16)
    d_model = x.shape[1]
    half = d_model // 2
    for c in range(2):
        br_p = _dot(h_p, wbrp_ref[:, c * half:(c + 1) * half])
        m_p = proj(C_MA + d_model + c * half, half)
        gp_ref[:, c * half:(c + 1) * half] = (jax.nn.sigmoid(m_p) * br_p).astype(jnp.bfloat16)
        m_a = proj(C_MA + c * half, half)
        sma_ref[:, c * half:(c + 1) * half] = jax.nn.sigmoid(m_a).astype(jnp.bfloat16)


def _inproj(x, scale, shift, hist, lw, *, layer, tm, tstride, tiles_per_seq, prompt, depth=None, kv_bufs=()):
    rows, d = x.shape
    n_tiles = rows // tm
    hist_rows = POOL_SLOTS * tstride
    bf, f32 = jnp.bfloat16, jnp.float32

    def full(a):
        return pl.BlockSpec(a.shape, lambda i, nd=a.ndim: (0,) * nd)

    def of_layer(a):
        return pl.BlockSpec((None,) + a.shape[1:], lambda i, nd=a.ndim: (layer,) + (0,) * (nd - 1))

    if scale.ndim == 3:
        mod_spec = pl.BlockSpec((None, 1, d), lambda i: (i // tiles_per_seq, 0, 0))
    else:
        mod_spec = pl.BlockSpec((tm, d), lambda i: (i, 0))

    def rowblk(width):
        return pl.BlockSpec((tm, width), lambda i: (i, 0))

    n_seq = n_tiles // tiles_per_seq
    w_att = N_HEADS * LANES if prompt else D_ATTN
    if prompt:
        kv_shape = jax.ShapeDtypeStruct((depth, n_seq, D_ATTN, tiles_per_seq * tm), f32)
        kv_spec = pl.BlockSpec((None, None, D_ATTN, tm),
                               lambda i: (layer, i // tiles_per_seq, 0, lax.rem(i, tiles_per_seq)))
        vb_shape = jax.ShapeDtypeStruct((n_seq, w_att, tiles_per_seq * tm), bf)
        vb_spec = pl.BlockSpec((None, w_att, tm), lambda i: (i // tiles_per_seq, 0, lax.rem(i, tiles_per_seq)))
    else:
        kv_shape = jax.ShapeDtypeStruct((rows, D_ATTN), f32)
        kv_spec = rowblk(D_ATTN)
        vb_shape = jax.ShapeDtypeStruct((rows, w_att), bf)
        vb_spec = rowblk(w_att)
    n_alias = len(kv_bufs)
    out_shape = (
        jax.ShapeDtypeStruct((rows, w_att), bf),
        kv_shape,
        jax.ShapeDtypeStruct((rows, w_att), bf),
        kv_shape,
        vb_shape,
        jax.ShapeDtypeStruct((rows, N_HEADS), f32),
        jax.ShapeDtypeStruct((rows, D_ATTN), bf),
        jax.ShapeDtypeStruct((rows, d), bf),
        jax.ShapeDtypeStruct((rows, d), bf),
        jax.ShapeDtypeStruct((n_seq, hist_rows, D_POOL), f32),
    )
    out_specs = (rowblk(w_att), kv_spec, rowblk(w_att), kv_spec, vb_spec,
                 rowblk(N_HEADS), rowblk(D_ATTN), rowblk(d), rowblk(d),
                 pl.BlockSpec((None, hist_rows, D_POOL), lambda i: (i // tiles_per_seq, 0, 0)))
    kern = functools.partial(_inproj_kernel, tm=tm, tstride=tstride, tiles_per_seq=tiles_per_seq,
                             prompt=prompt, n_alias=n_alias)
    params = (lw["norm_g"], lw["w_qkv"], lw["w_rest"], lw["w_f"], lw["b_f"], lw["q_gain"], lw["k_gain"])
    params2 = (lw["w_pool"], lw["pool_scale"], lw["w_br_p"])
    operands = (x, scale, shift) + params + (hist,) + params2
    assert len(operands) == N_INPROJ_IN
    return pl.pallas_call(
        kern,
        grid=(n_tiles,),
        in_specs=[rowblk(d), mod_spec, mod_spec] + [of_layer(a) for a in params] + [full(hist)]
                 + [of_layer(a) for a in params2] + [pl.BlockSpec(memory_space=pl.ANY)] * n_alias,
        out_specs=out_specs,
        out_shape=out_shape,
        input_output_aliases={N_INPROJ_IN: 1, N_INPROJ_IN + 1: 3} if n_alias else {},
        scratch_shapes=[pltpu.VMEM((hist_rows + tm, D_POOL), f32), pltpu.VMEM((1, F_PAD), f32)],
        compiler_params=pltpu.CompilerParams(dimension_semantics=("arbitrary",), vmem_limit_bytes=VMEM_LIMIT),
        name="in_proj",
    )(*operands, *kv_bufs)


def _attn_kernel(q_ref, k_ref, v_ref, o_ref, m_scr, acc_scr, s_scr, *, tq, hg):
    qi = pl.program_id(2)
    row = lax.broadcasted_iota(jnp.int32, (tq, tq), 0)
    col = lax.broadcasted_iota(jnp.int32, (tq, tq), 1)
    nt = (((1,), (1,)), ((), ()))
    m_scr[...] = jnp.full_like(m_scr, -jnp.inf)
    acc_scr[...] = jnp.zeros_like(acc_scr)

    def scores(kt, hh):
        start = pl.multiple_of(kt * tq, tq)
        sl = slice(hh * LANES, (hh + 1) * LANES)
        return lax.dot_general(k_ref[pl.ds(start, tq), sl], q_ref[:, sl], nt, preferred_element_type=jnp.float32)

    def accumulate(s, kt, hh, masked):
        start = pl.multiple_of(kt * tq, tq)
        sl = slice(hh * LANES, (hh + 1) * LANES)
        if masked:
            s = jnp.where(row <= col, s, -jnp.inf)
        m_prev = m_scr[hh]
        m_next = jnp.maximum(m_prev, jnp.max(s, axis=0, keepdims=True))
        p = jnp.exp2(s - m_next)
        alpha = jnp.exp2(m_prev - m_next)
        acc_scr[hh] = alpha * acc_scr[hh] + _dot(v_ref[sl, pl.ds(start, tq)], p.astype(jnp.bfloat16))
        m_scr[hh] = m_next

    for hh in range(hg):
        s_scr[hh] = scores(0, hh)

    def body(kt, carry):
        for hh in range(hg):
            s = s_scr[hh]
            s_scr[hh] = scores(kt + 1, hh)
            accumulate(s, kt, hh, False)
        return carry

    lax.fori_loop(0, qi, body, 0)
    for hh in range(hg):
        accumulate(s_scr[hh], qi, hh, True)
    lane = lax.broadcasted_iota(jnp.int32, (tq, LANES), 1)
    for j in range(hg // 2):
        a0, a1 = acc_scr[2 * j], acc_scr[2 * j + 1]
        o0 = (a0 / a0[HEAD_DIM:HEAD_DIM + 1, :]).T
        o1 = (a1 / a1[HEAD_DIM:HEAD_DIM + 1, :]).T
        o_ref[:, j * LANES:(j + 1) * LANES] = jnp.where(lane < HEAD_DIM, o0,
                                                        pltpu.roll(o1, HEAD_DIM, axis=1)).astype(o_ref.dtype)


def _prompt_attention(q, k, v, *, tq, hg):
    b, t, _ = q.shape
    n_t = t // tq
    group = hg * LANES
    kern = functools.partial(_attn_kernel, tq=tq, hg=hg)
    return pl.pallas_call(
        kern,
        grid=(b, N_HEADS // hg, n_t),
        in_specs=[pl.BlockSpec((None, tq, group), lambda bi, hp, qi: (bi, qi, hp)),
                  pl.BlockSpec((None, t, group), lambda bi, hp, qi: (bi, 0, hp)),
                  pl.BlockSpec((None, group, t), lambda bi, hp, qi: (bi, hp, 0))],
        out_specs=pl.BlockSpec((None, tq, hg * HEAD_DIM), lambda bi, hp, qi: (bi, qi, hp)),
        out_shape=jax.ShapeDtypeStruct((b, t, D_ATTN), jnp.bfloat16),
        scratch_shapes=[pltpu.VMEM((hg, 1, tq), jnp.float32), pltpu.VMEM((hg, LANES, tq), jnp.float32),
                        pltpu.VMEM((hg, tq, tq), jnp.float32)],
        compiler_params=pltpu.CompilerParams(dimension_semantics=("arbitrary", "arbitrary", "arbitrary"),
                                             vmem_limit_bytes=VMEM_LIMIT),
        name="fox_prompt_attn",
    )(q, k, v)


def _sample_attn_kernel(pt_ref, q_ref, kn_ref, vn_ref, lfn_ref, lower_ref, lfc_hbm, kc_hbm, vc_hbm,
                        o_ref, kbuf, vbuf, lbuf, g_scr, sem, *, layer, n_pages, s_new):
    f32, bf = jnp.float32, jnp.bfloat16
    nt = (((1,), (1,)), ((), ()))
    page = lfn_ref.shape[-1]
    b = pl.program_id(0)
    n_b = pl.num_programs(0)
    slot = lax.rem(b, 2)

    def page_copies(seq, slot_):
        out = []
        for p in range(n_pages):
            pg = pt_ref[seq, p]
            lanes = pl.ds(p * page, page)
            out.append(pltpu.make_async_copy(lfc_hbm.at[layer, pg], lbuf.at[slot_, :, lanes], sem.at[0, slot_]))
            out.append(pltpu.make_async_copy(kc_hbm.at[layer, pg], kbuf.at[slot_, :, :, lanes], sem.at[1, slot_]))
            out.append(pltpu.make_async_copy(vc_hbm.at[layer, pg], vbuf.at[slot_, :, :, lanes], sem.at[2, slot_]))
        return out

    @pl.when(b == 0)
    def _():
        for c in page_copies(0, 0):
            c.start()

    @pl.when(b + 1 < n_b)
    def _():
        for c in page_copies(b + 1, 1 - slot):
            c.start()

    for c in page_copies(b, slot):
        c.wait()

    carry = jnp.zeros((N_HEADS, 1), f32)
    for p in reversed(range(n_pages)):
        lf = lbuf[slot, :, p * page:(p + 1) * page]
        g_scr[:, p * page:(p + 1) * page] = (_dot_exact_lhs(lf, lower_ref[...]) + carry) * LOG2E
        carry = carry + jnp.sum(lf, axis=1, keepdims=True)

    n_new = kn_ref.shape[0]
    qi = lax.broadcasted_iota(jnp.int32, (Q_ROWS, n_new), 0)
    ti = lax.broadcasted_iota(jnp.int32, (Q_ROWS, n_new), 1)
    new_ok = (ti <= qi) & (ti < s_new)
    q_row = lax.broadcasted_iota(jnp.int32, (Q_ROWS, 1), 0)
    t_lane = lax.broadcasted_iota(jnp.int32, (1, n_new), 1)
    for h in range(N_HEADS):
        qh = q_ref[h]
        kn_h = kn_ref[:, h * HEAD_DIM:(h + 1) * HEAD_DIM]
        vn_h = vn_ref[:, h * HEAD_DIM:(h + 1) * HEAD_DIM]
        lf_new = lfn_ref[h:h + 1, :]
        c_col = jnp.zeros((Q_ROWS, 1), f32)
        c_row = jnp.zeros((1, n_new), f32)
        for t_i in range(s_new):
            lf_t = lf_new[:, t_i:t_i + 1]
            c_col = c_col + jnp.where(q_row >= t_i, lf_t, 0.0)
            c_row = c_row + jnp.where(t_lane >= t_i, lf_t, 0.0)
        c_col = c_col * LOG2E
        c_row = c_row * LOG2E
        s_n = lax.dot_general(qh, kn_h, nt, preferred_element_type=f32)
        s_n = jnp.where(new_ok, s_n + c_col - c_row, -jnp.inf)
        kt = kbuf[slot, h].astype(bf)
        vt = vbuf[slot, h].astype(bf)
        s = _dot(qh, kt) + c_col + g_scr[h:h + 1, :]
        m = jnp.maximum(jnp.max(s, axis=1, keepdims=True), jnp.max(s_n, axis=1, keepdims=True))
        pr = jnp.exp2(s - m)
        pr_n = jnp.exp2(s_n - m)
        l = jnp.sum(pr, axis=1, keepdims=True) + jnp.sum(pr_n, axis=1, keepdims=True)
        acc = (lax.dot_general(pr.astype(bf), vt, nt, preferred_element_type=f32)
               + _dot(pr_n.astype(bf), vn_h))
        o_ref[h] = acc / l


def _sample_attention(page_table, q16, kn_t, vn_t, lf_new, lower, logf_t, cache_kt, cache_vt, layer, *, s_new):
    db, n_pages = page_table.shape
    heads, hd, page = cache_kt.shape[2:]
    past = n_pages * page
    f32 = jnp.float32

    def per_b(a):
        return pl.BlockSpec((None,) + a.shape[1:], lambda b, pt, nd=a.ndim: (b,) + (0,) * (nd - 1))

    hbm = pl.BlockSpec(memory_space=pl.ANY)
    grid_spec = pltpu.PrefetchScalarGridSpec(
        num_scalar_prefetch=1,
        grid=(db,),
        in_specs=[per_b(q16), per_b(kn_t), per_b(vn_t), per_b(lf_new),
                  pl.BlockSpec(lower.shape, lambda b, pt: (0, 0)), hbm, hbm, hbm],
        out_specs=pl.BlockSpec((None, heads, Q_ROWS, hd), lambda b, pt: (b, 0, 0, 0)),
        scratch_shapes=[pltpu.VMEM((2, heads, hd, past), f32), pltpu.VMEM((2, heads, hd, past), f32),
                        pltpu.VMEM((2, heads, past), f32), pltpu.VMEM((heads, past), f32),
                        pltpu.SemaphoreType.DMA((3, 2))],
    )
    kern = functools.partial(_sample_attn_kernel, layer=layer, n_pages=n_pages, s_new=s_new)
    return pl.pallas_call(
        kern,
        grid_spec=grid_spec,
        out_shape=jax.ShapeDtypeStruct((db, heads, Q_ROWS, hd), f32),
        compiler_params=pltpu.CompilerParams(dimension_semantics=("arbitrary",), vmem_limit_bytes=VMEM_LIMIT),
        name="fox_sample_attn",
    )(page_table, q16, kn_t, vn_t, lf_new, lower, logf_t, cache_kt, cache_vt)


def _out_kernel(x_ref, o_ref, ga_ref, sma_ref, gp_ref, gate_ref, wbra_ref, wout_ref, y_ref):
    h_a = (o_ref[...].astype(jnp.float32) * ga_ref[...].astype(jnp.float32)).astype(jnp.bfloat16)
    br_a = _dot(h_a, wbra_ref[...])
    merged = sma_ref[...].astype(jnp.float32) * br_a + gp_ref[...].astype(jnp.float32)
    y_ref[...] = x_ref[...] + gate_ref[...] * _dot(merged.astype(jnp.bfloat16), wout_ref[...])


def _out_proj(x, o, ga, sma, gp, gate, w_br_a, w_out, *, layer, tm, tiles_per_seq):
    rows, d = x.shape

    def of_layer(a):
        return pl.BlockSpec((None,) + a.shape[1:], lambda i: (layer, 0, 0))

    def rowblk(width):
        return pl.BlockSpec((tm, width), lambda i: (i, 0))

    if gate.ndim == 3:
        gate_spec = pl.BlockSpec((None, 1, d), lambda i: (i // tiles_per_seq, 0, 0))
    else:
        gate_spec = pl.BlockSpec((tm, d), lambda i: (i, 0))
    return pl.pallas_call(
        _out_kernel,
        grid=(rows // tm,),
        in_specs=[rowblk(d), rowblk(D_ATTN), rowblk(D_ATTN), rowblk(d), rowblk(d), gate_spec,
                  of_layer(w_br_a), of_layer(w_out)],
        out_specs=rowblk(d),
        out_shape=jax.ShapeDtypeStruct((rows, d), jnp.float32),
        compiler_params=pltpu.CompilerParams(dimension_semantics=("arbitrary",), vmem_limit_bytes=VMEM_LIMIT),
        name="out_proj",
    )(x, o, ga, sma, gp, gate, w_br_a, w_out)


def kernel(x_prompt, x_sample, cache_k, cache_v, cache_logf, state_pool, page_table, c_prompt, c_sample,
           norm_g, w_ada, b_ada, w_in, b_f, q_norm_g, k_norm_g, w_pool_grp, pool_scale, w_br_a, w_br_p, w_out):
    f32, bf = jnp.float32, jnp.bfloat16
    b, t, d = x_prompt.shape
    db, s_new, _ = x_sample.shape
    depth = w_in.shape[0]
    page = cache_k.shape[2]
    n_pages = page_table.shape[1]
    tm = 512
    tq = 512
    assert t % tm == 0 and t % tq == 0 and db % 8 == 0 and s_new <= Q_ROWS and n_pages == page_table.shape[1]

    o_q, o_k, o_v = 0, D_ATTN, 2 * D_ATTN
    o_f = 3 * D_ATTN
    o_ga = o_f + N_HEADS
    o_z = o_ga + D_ATTN
    o_gp = o_z + D_POOL
    o_m = o_gp + D_POOL
    lw = {
        "w_qkv": w_in[:, :, o_q:o_f].astype(bf),
        "w_rest": w_in[:, :, o_ga:].astype(bf),
        "w_f": jnp.pad(w_in[:, :, o_f:o_ga], ((0, 0), (0, 0), (0, F_PAD - N_HEADS))).astype(bf),
        "b_f": jnp.pad(b_f, ((0, 0), (0, F_PAD - N_HEADS))).reshape(depth, 1, F_PAD),
        "norm_g": norm_g.reshape(depth, 1, d),
        "q_gain": jnp.tile(q_norm_g, (1, N_HEADS)).reshape(depth, 1, D_ATTN),
        "k_gain": jnp.tile(k_norm_g, (1, N_HEADS)).reshape(depth, 1, D_ATTN),
        "w_pool": w_pool_grp.astype(bf),
        "pool_scale": pool_scale.reshape(depth, 1, D_POOL),
        "w_br_p": w_br_p.astype(bf),
    }
    w_br_a_bf, w_out_bf = w_br_a.astype(bf), w_out.astype(bf)

    rp =lax.broadcasted_iota(jnp.int32, (page, page), 0)
    cp = lax.broadcasted_iota(jnp.int32, (page, page), 1)
    lower = (rp > cp).astype(bf)
    cache_kt = jnp.transpose(cache_k, (0, 1, 3, 4, 2))
    cache_vt = jnp.transpose(cache_v, (0, 1, 3, 4, 2))
    logf_t = jnp.transpose(cache_logf, (0, 1, 3, 2))

    mc = db + 8
    c_all = jnp.concatenate([c_sample, c_prompt, jnp.zeros((mc - db - b, d), f32)], axis=0)
    mod = _modulation(c_all, w_ada, b_ada)

    hp = x_prompt.reshape(b * t, d)
    hs = jnp.transpose(x_sample, (1, 0, 2)).reshape(s_new * db, d)
    hist_zero = jnp.zeros((POOL_SLOTS, D_POOL), f32)
    tiles_per_seq = t // tm

    outs = {n: [] for n in ("kp", "vp", "fp", "pp", "ks", "vs", "fs", "ps")}
    for l in range(depth):
        mod_s = mod[l, :db]
        mod_p = mod[l, db:db + b]
        shift_p, scale_p, gate_p = (mod_p[:, j * d:(j + 1) * d].reshape(b, 1, d) for j in range(3))
        mod_s_rows = jnp.tile(mod_s, (s_new, 1))
        shift_s, scale_s, gate_s = (mod_s_rows[:, j * d:(j + 1) * d] for j in range(3))

        (q_p, k_stack, kb_p, v_stack, vb_p, lf_p, ga_p, sma_p, gp_p, tail_p) = _inproj(
            hp, scale_p, shift_p, hist_zero, lw, layer=l, tm=tm, tstride=1, tiles_per_seq=tiles_per_seq,
            prompt=True, depth=depth, kv_bufs=() if l == 0 else (k_stack, v_stack))
        w_aug = N_HEADS * LANES
        o_p = _prompt_attention(q_p.reshape(b, t, w_aug), kb_p.reshape(b, t, w_aug), vb_p, tq=tq, hg=4)
        hp = _out_proj(hp, o_p.reshape(b * t, D_ATTN), ga_p, sma_p, gp_p, gate_p, w_br_a_bf, w_out_bf,
                       layer=l, tm=tm, tiles_per_seq=tiles_per_seq)
        outs["fp"].append(lf_p.reshape(b, t, N_HEADS))
        outs["pp"].append(tail_p[:, 1:, :])

        hist_s = jnp.concatenate([jnp.zeros((1, db, D_POOL), f32), jnp.transpose(state_pool[l], (1, 0, 2))],
                                 axis=0).reshape(POOL_SLOTS * db, D_POOL)
        (q_s, k32_s, kb_s, v32_s, vb_s, lf_s, ga_s, sma_s, gp_s, tail_s) = _inproj(
            hs, scale_s, shift_s, hist_s, lw, layer=l, tm=s_new * db, tstride=db, tiles_per_seq=1,
            prompt=False)
        q16 = jnp.transpose(q_s.reshape(s_new, db, N_HEADS, HEAD_DIM), (1, 2, 0, 3))
        q16 = jnp.pad(q16, ((0, 0), (0, 0), (0, Q_ROWS - s_new), (0, 0)))
        kn_t = jnp.pad(jnp.transpose(kb_s.reshape(s_new, db, D_ATTN), (1, 0, 2)),
                       ((0, 0), (0, Q_ROWS - s_new), (0, 0)))
        vn_t = jnp.pad(jnp.transpose(vb_s.reshape(s_new, db, D_ATTN), (1, 0, 2)),
                       ((0, 0), (0, Q_ROWS - s_new), (0, 0)))
        lf_bt = jnp.transpose(lf_s.reshape(s_new, db, N_HEADS), (1, 0, 2))
        lf_bh = jnp.transpose(lf_bt, (0, 2, 1))
        lf_new = jnp.pad(lf_bh, ((0, 0), (0, 0), (0, page - s_new)))
        o_s = _sample_attention(page_table, q16, kn_t, vn_t, lf_new, lower, logf_t, cache_kt, cache_vt, l,
                                s_new=s_new)
        o_s = jnp.transpose(o_s[:, :, :s_new, :], (2, 0, 1, 3)).reshape(s_new * db, D_ATTN).astype(bf)
        hs = _out_proj(hs, o_s, ga_s, sma_s, gp_s, gate_s, w_br_a_bf, w_out_bf, layer=l, tm=s_new * db,
                       tiles_per_seq=1)
        outs["ks"].append(jnp.transpose(k32_s.reshape(s_new, db, N_HEADS, HEAD_DIM), (1, 0, 2, 3)))
        outs["vs"].append(jnp.transpose(v32_s.reshape(s_new, db, N_HEADS, HEAD_DIM), (1, 0, 2, 3)))
        outs["fs"].append(lf_bt)
        outs["ps"].append(jnp.transpose(tail_s.reshape(POOL_SLOTS, db, D_POOL)[1:], (1, 0, 2)))

    y_p = hp.reshape(b, t, d)
    y_s = jnp.transpose(hs.reshape(s_new, db, d), (1, 0, 2))
    k_p = jnp.transpose(k_stack.reshape(depth, b, N_HEADS, HEAD_DIM, t), (0, 1, 4, 2, 3))
    v_p = jnp.transpose(v_stack.reshape(depth, b, N_HEADS, HEAD_DIM, t), (0, 1, 4, 2, 3))
    return (y_p, y_s, k_p, v_p, jnp.stack(outs["fp"]),
            jnp.stack(outs["pp"]), jnp.stack(outs["ks"]), jnp.stack(outs["vs"]), jnp.stack(outs["fs"]),
            jnp.stack(outs["ps"]))
```

```python
import functools
import math

import jax
import jax.numpy as jnp
from jax import lax
from jax.experimental import pallas as pl
from jax.experimental.pallas import tpu as pltpu

N_HEADS = 8
HEAD_DIM = 64
D_ATTN = N_HEADS * HEAD_DIM
POOL_WINDOWS = (2, 4, 8, 16)
POOL_GC = 128
D_POOL = POOL_GC * len(POOL_WINDOWS)
POOL_HIST = max(POOL_WINDOWS) - 1
POOL_SLOTS = POOL_HIST + 1
EPS = 1e-6
LOG2E = math.log2(math.e)
LANES = 128
F_PAD = LANES
Q_ROWS = 16
VMEM_LIMIT = 56 * 1024 * 1024

C_Q, C_K, C_V, C_GA, C_Z, C_GP = 0, 512, 1024, 1536, 2048, 2560
C_MA = 3072


def _split3(x):
    hi = x.astype(jnp.bfloat16)
    r1 = x - hi.astype(jnp.float32)
    mid = r1.astype(jnp.bfloat16)
    lo = (r1 - mid.astype(jnp.float32)).astype(jnp.bfloat16)
    return hi, mid, lo


def _dot(a, b):
    return jnp.dot(a, b, preferred_element_type=jnp.float32)


def _dot_exact_lhs(x, sel):
    hi, mid, lo = _split3(x)
    return _dot(hi, sel) + _dot(mid, sel) + _dot(lo, sel)


def _silu(x):
    return x * jax.nn.sigmoid(x)


def _log_sigmoid(x):
    return jnp.minimum(x, 0.0) - jnp.log1p(jnp.exp(-jnp.abs(x)))


def _mod_kernel(c_ref, w_ref, b_ref, o_ref):
    a_hi, a_mid, a_lo = _split3(_silu(c_ref[...]))
    w_hi, w_mid, w_lo = _split3(w_ref[...])
    acc = _dot(a_hi, w_hi) + (_dot(a_hi, w_mid) + _dot(a_mid, w_hi))
    acc = acc + (_dot(a_hi, w_lo) + _dot(a_lo, w_hi) + _dot(a_mid, w_mid))
    o_ref[...] = acc + b_ref[...]


def _modulation(c_all, w_ada_bf, b_ada):
    depth, d, d3 = w_ada_bf.shape
    mc = c_all.shape[0]
    tn = 512
    return pl.pallas_call(
        _mod_kernel,
        grid=(depth, d3 // tn),
        in_specs=[pl.BlockSpec((mc, d), lambda l, j: (0, 0)),
                  pl.BlockSpec((None, d, tn), lambda l, j: (l, 0, j)),
                  pl.BlockSpec((None, 1, tn), lambda l, j: (l, 0, j))],
        out_specs=pl.BlockSpec((None, mc, tn), lambda l, j: (l, 0, j)),
        out_shape=jax.ShapeDtypeStruct((depth, mc, d3), jnp.float32),
        name="adaln_mod",
    )(c_all, w_ada_bf, b_ada.reshape(depth, 1, d3))


N_INPROJ_IN = 14


def _cumsum_rows(x):
    n = x.shape[0]
    row = lax.broadcasted_iota(jnp.int32, x.shape, 0)
    step = 1
    while step < n:
        x = x + jnp.where(row >= step, pltpu.roll(x, step, axis=0), 0.0)
        step *= 2
    return x


def _inproj_kernel(*refs, tm, tstride, tiles_per_seq, prompt, n_alias):
    (x_ref, scale_ref, shift_ref, g_ref, wqkv_ref, wrest_ref, wf_ref, bf_ref, qg_ref, kg_ref,
     hist_ref, wpool_ref, pscale_ref, wbrp_ref) = refs[:N_INPROJ_IN]
    (q_ref, k32_ref, kb_ref, v32_ref, vb_ref, logf_ref, ga_ref, sma_ref, gp_ref, tail_ref,
     zext_scr, fcarry_scr) = refs[N_INPROJ_IN + n_alias:]
    hist_rows = POOL_SLOTS * tstride
    i = pl.program_id(0)
    tile_in_seq = lax.rem(i, tiles_per_seq)
    first = tile_in_seq == 0

    @pl.when(first)
    def _():
        zext_scr[0:hist_rows, :] = hist_ref[...]
        fcarry_scr[...] = jnp.zeros_like(fcarry_scr)

    x = x_ref[...]
    ms = jnp.mean(x * x, axis=-1, keepdims=True)
    xn = x * lax.rsqrt(ms + EPS) * g_ref[...]
    u = (xn * (1.0 + scale_ref[...]) + shift_ref[...]).astype(jnp.bfloat16)

    c_f = C_GA + wrest_ref.shape[1]

    def proj(c0, width):
        if c0 < C_GA:
            return _dot(u, wqkv_ref[:, c0:c0 + width])
        if c0 < c_f:
            return _dot(u, wrest_ref[:, c0 - C_GA:c0 - C_GA + width])
        return _dot(u, wf_ref[...])

    lane = lax.broadcasted_iota(jnp.int32, (tm, LANES), 1)
    low_half = lane < HEAD_DIM

    def head_norm(p, gain_ref):
        outs = []
        for j in range(D_ATTN // LANES):
            slab = p[:, j * LANES:(j + 1) * LANES]
            sq = slab * slab
            s_lo = jnp.sum(jnp.where(low_half, sq, 0.0), axis=1, keepdims=True)
            s_hi = jnp.sum(jnp.where(low_half, 0.0, sq), axis=1, keepdims=True)
            r = jnp.where(low_half, lax.rsqrt(s_lo * (1.0 / HEAD_DIM) + EPS), lax.rsqrt(s_hi * (1.0 / HEAD_DIM) + EPS))
            outs.append(slab * r)
        return jnp.concatenate(outs, axis=1) * gain_ref[...]

    logf =_log_sigmoid(proj(c_f, F_PAD) + bf_ref[...])
    logf_ref[...] = logf[:, :N_HEADS]

    qn = head_norm(proj(C_Q, D_ATTN), qg_ref) * (HEAD_DIM ** -0.5 * LOG2E)
    kn = head_norm(proj(C_K, D_ATTN), kg_ref)
    pv = proj(C_V, D_ATTN)
    if prompt:
        pv_t = pv.T
        k32_ref[...] = kn.T
        v32_ref[...] = pv_t
        fc = _cumsum_rows(logf) + fcarry_scr[...]
        fcarry_scr[...] = fc[tm - 1:tm, :]
        ones_row = jnp.where(lax.broadcasted_iota(jnp.int32, (HEAD_DIM, tm), 0) == 0, 1.0, 0.0)
        pieces = [p.astype(jnp.float32) for p in _split3(fc * LOG2E)]
        for h in range(N_HEADS):
            j = h // 2
            f_hi, f_mid, f_lo = (p[:, h:h + 1] for p in pieces)

            def head_tile(x, h=h, j=j):
                slab = x[:, j * LANES:(j + 1) * LANES]
                return slab if h % 2 == 0 else pltpu.roll(slab, HEAD_DIM, axis=1)

            def tail(a, b, c, rest, first):
                return jnp.where(lane == first, a, jnp.where(lane == first + 1, b,
                                                             jnp.where(lane == first + 2, c, rest)))

            ones_q = jnp.where(lane < HEAD_DIM + 3, 1.0, 0.0)
            ones_k = jnp.where((lane >= HEAD_DIM + 3) & (lane < HEAD_DIM + 6), 1.0, 0.0)
            q_t = jnp.where(low_half, head_tile(qn), tail(f_hi, f_mid, f_lo, ones_q, HEAD_DIM + 3))
            k_t = jnp.where(low_half, head_tile(kn), tail(-f_hi, -f_mid, -f_lo, ones_k, HEAD_DIM))
            q_ref[:, h * LANES:(h + 1) * LANES] = q_t.astype(jnp.bfloat16)
            kb_ref[:, h * LANES:(h + 1) * LANES] = k_t.astype(jnp.bfloat16)
            vb_ref[h * LANES:h * LANES + HEAD_DIM, :] = pv_t[h * HEAD_DIM:(h + 1) * HEAD_DIM, :].astype(jnp.bfloat16)
            vb_ref[h * LANES + HEAD_DIM:(h + 1) * LANES, :] = ones_row.astype(jnp.bfloat16)
    else:
        k32_ref[...] = kn
        v32_ref[...] = pv
        q_ref[...] = qn.astype(jnp.bfloat16)
        kb_ref[...] = kn.astype(jnp.bfloat16)
        vb_ref[...] = pv.astype(jnp.bfloat16)

    ga_ref[...] = _silu(proj(C_GA, D_ATTN)).astype(jnp.bfloat16)

    zext_scr[hist_rows:hist_rows + tm, :] = proj(C_Z, D_POOL)
    if prompt:
        pos = tile_in_seq * tm + lax.broadcasted_iota(jnp.int32, (tm, 1), 0)
    ys = []
    for g, w in enumerate(POOL_WINDOWS):
        col = zext_scr[:, g * POOL_GC:(g + 1) * POOL_GC]
        s = col
        span = 1
        while span < w:
            s = s + pltpu.roll(s, span * tstride, axis=0)
            span *= 2
        if prompt:
            cnt = jnp.minimum(pos + 1, w).astype(jnp.float32)
        else:
            cnt = jnp.float32(w)
        diff = s[hist_rows:, :] / cnt - col[hist_rows:, :]
        ys.append(_dot(diff.astype(jnp.bfloat16), wpool_ref[g]))
    y_pool = jnp.concatenate(ys, axis=1) * pscale_ref[...]
    tail = zext_scr[tm:tm + hist_rows, :]
    tail_ref[...] = tail
    if tiles_per_seq > 1:
        zext_scr[0:hist_rows, :] = tail

    h_p = (y_pool * _silu(proj(C_GP, D_POOL))).astype(jnp.bfloat16)
    d_model = x.shape[1]
    half = d_model // 2
    for c in range(2):
        br_p = _dot(h_p, wbrp_ref[:, c * half:(c + 1) * half])
        m_p = proj(C_MA + d_model + c * half, half)
        gp_ref[:, c * half:(c + 1) * half] = (jax.nn.sigmoid(m_p) * br_p).astype(jnp.bfloat16)
        m_a = proj(C_MA + c * half, half)
        sma_ref[:, c * half:(c + 1) * half] = jax.nn.sigmoid(m_a).astype(jnp.bfloat16)


def _inproj(x, scale, shift, hist, lw, *, layer, tm, tstride, tiles_per_seq, prompt, depth=None, kv_bufs=()):
    rows, d = x.shape
    n_tiles = rows // tm
    hist_rows = POOL_SLOTS * tstride
    bf, f32 = jnp.bfloat16, jnp.float32

    def full(a):
        return pl.BlockSpec(a.shape, lambda i, nd=a.ndim: (0,) * nd)

    def of_layer(a):
        return pl.BlockSpec((None,) + a.shape[1:], lambda i, nd=a.ndim: (layer,) + (0,) * (nd - 1))

    if scale.ndim == 3:
        mod_spec = pl.BlockSpec((None, 1, d), lambda i: (i // tiles_per_seq, 0, 0))
    else:
        mod_spec = pl.BlockSpec((tm, d), lambda i: (i, 0))

    def rowblk(width):
        return pl.BlockSpec((tm, width), lambda i: (i, 0))

    n_seq = n_tiles // tiles_per_seq
    w_att = N_HEADS * LANES if prompt else D_ATTN
    if prompt:
        kv_shape = jax.ShapeDtypeStruct((depth, n_seq, D_ATTN, tiles_per_seq * tm), f32)
        kv_spec = pl.BlockSpec((None, None, D_ATTN, tm),
                               lambda i: (layer, i // tiles_per_seq, 0, lax.rem(i, tiles_per_seq)))
        vb_shape = jax.ShapeDtypeStruct((n_seq, w_att, tiles_per_seq * tm), bf)
        vb_spec = pl.BlockSpec((None, w_att, tm), lambda i: (i // tiles_per_seq, 0, lax.rem(i, tiles_per_seq)))
    else:
        kv_shape = jax.ShapeDtypeStruct((rows, D_ATTN), f32)
        kv_spec = rowblk(D_ATTN)
        vb_shape = jax.ShapeDtypeStruct((rows, w_att), bf)
        vb_spec = rowblk(w_att)
    n_alias = len(kv_bufs)
    out_shape = (
        jax.ShapeDtypeStruct((rows, w_att), bf),
        kv_shape,
        jax.ShapeDtypeStruct((rows, w_att), bf),
        kv_shape,
        vb_shape,
        jax.ShapeDtypeStruct((rows, N_HEADS), f32),
        jax.ShapeDtypeStruct((rows, D_ATTN), bf),
        jax.ShapeDtypeStruct((rows, d), bf),
        jax.ShapeDtypeStruct((rows, d), bf),
        jax.ShapeDtypeStruct((n_seq, hist_rows, D_POOL), f32),
    )
    out_specs = (rowblk(w_att), kv_spec, rowblk(w_att), kv_spec, vb_spec,
                 rowblk(N_HEADS), rowblk(D_ATTN), rowblk(d), rowblk(d),
                 pl.BlockSpec((None, hist_rows, D_POOL), lambda i: (i // tiles_per_seq, 0, 0)))
    kern = functools.partial(_inproj_kernel, tm=tm, tstride=tstride, tiles_per_seq=tiles_per_seq,
                             prompt=prompt, n_alias=n_alias)
    params = (lw["norm_g"], lw["w_qkv"], lw["w_rest"], lw["w_f"], lw["b_f"], lw["q_gain"], lw["k_gain"])
    params2 = (lw["w_pool"], lw["pool_scale"], lw["w_br_p"])
    operands = (x, scale, shift) + params + (hist,) + params2
    assert len(operands) == N_INPROJ_IN
    return pl.pallas_call(
        kern,
        grid=(n_tiles,),
        in_specs=[rowblk(d), mod_spec, mod_spec] + [of_layer(a) for a in params] + [full(hist)]
                 + [of_layer(a) for a in params2] + [pl.BlockSpec(memory_space=pl.ANY)] * n_alias,
        out_specs=out_specs,
        out_shape=out_shape,
        input_output_aliases={N_INPROJ_IN: 1, N_INPROJ_IN + 1: 3} if n_alias else {},
        scratch_shapes=[pltpu.VMEM((hist_rows + tm, D_POOL), f32), pltpu.VMEM((1, F_PAD), f32)],
        compiler_params=pltpu.CompilerParams(dimension_semantics=("arbitrary",), vmem_limit_bytes=VMEM_LIMIT),
        name="in_proj",
    )(*operands, *kv_bufs)


def _attn_kernel(q_ref, k_ref, v_ref, o_ref, m_scr, acc_scr, s_scr, *, tq, hg):
    qi = pl.program_id(2)
    row = lax.broadcasted_iota(jnp.int32, (tq, tq), 0)
    col = lax.broadcasted_iota(jnp.int32, (tq, tq), 1)
    nt = (((1,), (1,)), ((), ()))
    m_scr[...] = jnp.full_like(m_scr, -jnp.inf)
    acc_scr[...] = jnp.zeros_like(acc_scr)

    def scores(kt, hh):
        start = pl.multiple_of(kt * tq, tq)
        sl = slice(hh * LANES, (hh + 1) * LANES)
        return lax.dot_general(k_ref[pl.ds(start, tq), sl], q_ref[:, sl], nt, preferred_element_type=jnp.float32)

    def accumulate(s, kt, hh, masked):
        start = pl.multiple_of(kt * tq, tq)
        sl = slice(hh * LANES, (hh + 1) * LANES)
        if masked:
            s = jnp.where(row <= col, s, -jnp.inf)
        m_prev = m_scr[hh]
        m_next = jnp.maximum(m_prev, jnp.max(s, axis=0, keepdims=True))
        p = jnp.exp2(s - m_next)
        alpha = jnp.exp2(m_prev - m_next)
        acc_scr[hh] = alpha * acc_scr[hh] + _dot(v_ref[sl, pl.ds(start, tq)], p.astype(jnp.bfloat16))
        m_scr[hh] = m_next

    for hh in range(hg):
        s_scr[hh] = scores(0, hh)

    def body(kt, carry):
        for hh in range(hg):
            s = s_scr[hh]
            s_scr[hh] = scores(kt + 1, hh)
            accumulate(s, kt, hh, False)
        return carry

    lax.fori_loop(0, qi, body, 0)
    for hh in range(hg):
        accumulate(s_scr[hh], qi, hh, True)
    lane = lax.broadcasted_iota(jnp.int32, (tq, LANES), 1)
    for j in range(hg // 2):
        a0, a1 = acc_scr[2 * j], acc_scr[2 * j + 1]
        o0 = (a0 / a0[HEAD_DIM:HEAD_DIM + 1, :]).T
        o1 = (a1 / a1[HEAD_DIM:HEAD_DIM + 1, :]).T
        o_ref[:, j * LANES:(j + 1) * LANES] = jnp.where(lane < HEAD_DIM, o0,
                                                        pltpu.roll(o1, HEAD_DIM, axis=1)).astype(o_ref.dtype)


def _prompt_attention(q, k, v, *, tq, hg):
    b, t, _ = q.shape
    n_t = t // tq
    group = hg * LANES
    kern = functools.partial(_attn_kernel, tq=tq, hg=hg)
    return pl.pallas_call(
        kern,
        grid=(b, N_HEADS // hg, n_t),
        in_specs=[pl.BlockSpec((None, tq, group), lambda bi, hp, qi: (bi, qi, hp)),
                  pl.BlockSpec((None, t, group), lambda bi, hp, qi: (bi, 0, hp)),
                  pl.BlockSpec((None, group, t), lambda bi, hp, qi: (bi, hp, 0))],
        out_specs=pl.BlockSpec((None, tq, hg * HEAD_DIM), lambda bi, hp, qi: (bi, qi, hp)),
        out_shape=jax.ShapeDtypeStruct((b, t, D_ATTN), jnp.bfloat16),
        scratch_shapes=[pltpu.VMEM((hg, 1, tq), jnp.float32), pltpu.VMEM((hg, LANES, tq), jnp.float32),
                        pltpu.VMEM((hg, tq, tq), jnp.float32)],
        compiler_params=pltpu.CompilerParams(dimension_semantics=("arbitrary", "arbitrary", "arbitrary"),
                                             vmem_limit_bytes=VMEM_LIMIT),
        name="fox_prompt_attn",
    )(q, k, v)


def _sample_attn_kernel(pt_ref, q_ref, kn_ref, vn_ref, lfn_ref, lower_ref, lfc_hbm, kc_hbm, vc_hbm,
                        o_ref, kbuf, vbuf, lbuf, g_scr, sem, *, layer, n_pages, s_new):
    f32, bf = jnp.float32, jnp.bfloat16
    nt = (((1,), (1,)), ((), ()))
    page = lfn_ref.shape[-1]
    b = pl.program_id(0)
    n_b = pl.num_programs(0)
    slot = lax.rem(b, 2)

    def page_copies(seq, slot_):
        out = []
        for p in range(n_pages):
            pg = pt_ref[seq, p]
            lanes = pl.ds(p * page, page)
            out.append(pltpu.make_async_copy(lfc_hbm.at[layer, pg], lbuf.at[slot_, :, lanes], sem.at[0, slot_]))
            out.append(pltpu.make_async_copy(kc_hbm.at[layer, pg], kbuf.at[slot_, :, :, lanes], sem.at[1, slot_]))
            out.append(pltpu.make_async_copy(vc_hbm.at[layer, pg], vbuf.at[slot_, :, :, lanes], sem.at[2, slot_]))
        return out

    def start_all(copies):
        for n, c in enumerate(copies):
            c.start(priority=n % 2)

    @pl.when(b == 0)
    def _():
        start_all(page_copies(0, 0))

    @pl.when(b + 1 < n_b)
    def _():
        start_all(page_copies(b + 1, 1 - slot))

    for c in page_copies(b, slot):
        c.wait()

    carry = jnp.zeros((N_HEADS, 1), f32)
    for p in reversed(range(n_pages)):
        lf = lbuf[slot, :, p * page:(p + 1) * page]
        g_scr[:, p * page:(p + 1) * page] = (_dot_exact_lhs(lf, lower_ref[...]) + carry) * LOG2E
        carry = carry + jnp.sum(lf, axis=1, keepdims=True)

    n_new = kn_ref.shape[0]
    qi = lax.broadcasted_iota(jnp.int32, (Q_ROWS, n_new), 0)
    ti = lax.broadcasted_iota(jnp.int32, (Q_ROWS, n_new), 1)
    new_ok = (ti <= qi) & (ti < s_new)
    q_row = lax.broadcasted_iota(jnp.int32, (Q_ROWS, 1), 0)
    t_lane = lax.broadcasted_iota(jnp.int32, (1, n_new), 1)
    for h in range(N_HEADS):
        qh = q_ref[h]
        kn_h = kn_ref[:, h * HEAD_DIM:(h + 1) * HEAD_DIM]
        vn_h = vn_ref[:, h * HEAD_DIM:(h + 1) * HEAD_DIM]
        lf_new = lfn_ref[h:h + 1, :]
        c_col = jnp.zeros((Q_ROWS, 1), f32)
        c_row = jnp.zeros((1, n_new), f32)
        for t_i in range(s_new):
            lf_t = lf_new[:, t_i:t_i + 1]
            c_col = c_col + jnp.where(q_row >= t_i, lf_t, 0.0)
            c_row = c_row + jnp.where(t_lane >= t_i, lf_t, 0.0)
        c_col = c_col * LOG2E
        c_row = c_row * LOG2E
        s_n = lax.dot_general(qh, kn_h, nt, preferred_element_type=f32)
        s_n = jnp.where(new_ok, s_n + c_col - c_row, -jnp.inf)
        kt = kbuf[slot, h].astype(bf)
        vt = vbuf[slot, h].astype(bf)
        s = _dot(qh, kt) + c_col + g_scr[h:h + 1, :]
        m = jnp.maximum(jnp.max(s, axis=1, keepdims=True), jnp.max(s_n, axis=1, keepdims=True))
        pr = jnp.exp2(s - m)
        pr_n = jnp.exp2(s_n - m)
        l = jnp.sum(pr, axis=1, keepdims=True) + jnp.sum(pr_n, axis=1, keepdims=True)
        acc = (lax.dot_general(pr.astype(bf), vt, nt, preferred_element_type=f32)
               + _dot(pr_n.astype(bf), vn_h))
        o_ref[h] = acc / l


def _sample_attention(page_table, q16, kn_t, vn_t, lf_new, lower, logf_t, cache_kt, cache_vt, layer, *, s_new):
    db, n_pages = page_table.shape
    heads, hd, page = cache_kt.shape[2:]
    past = n_pages * page
    f32 = jnp.float32

    def per_b(a):
        return pl.BlockSpec((None,) + a.shape[1:], lambda b, pt, nd=a.ndim: (b,) + (0,) * (nd - 1))

    hbm = pl.BlockSpec(memory_space=pl.ANY)
    grid_spec = pltpu.PrefetchScalarGridSpec(
        num_scalar_prefetch=1,
        grid=(db,),
        in_specs=[per_b(q16), per_b(kn_t), per_b(vn_t), per_b(lf_new),
                  pl.BlockSpec(lower.shape, lambda b, pt: (0, 0)), hbm, hbm, hbm],
        out_specs=pl.BlockSpec((None, heads, Q_ROWS, hd), lambda b, pt: (b, 0, 0, 0)),
        scratch_shapes=[pltpu.VMEM((2, heads, hd, past), f32), pltpu.VMEM((2, heads, hd, past), f32),
                        pltpu.VMEM((2, heads, past), f32), pltpu.VMEM((heads, past), f32),
                        pltpu.SemaphoreType.DMA((3, 2))],
    )
    kern = functools.partial(_sample_attn_kernel, layer=layer, n_pages=n_pages, s_new=s_new)
    return pl.pallas_call(
        kern,
        grid_spec=grid_spec,
        out_shape=jax.ShapeDtypeStruct((db, heads, Q_ROWS, hd), f32),
        compiler_params=pltpu.CompilerParams(dimension_semantics=("arbitrary",), vmem_limit_bytes=VMEM_LIMIT),
        name="fox_sample_attn",
    )(page_table, q16, kn_t, vn_t, lf_new, lower, logf_t, cache_kt, cache_vt)


def _out_kernel(x_ref, o_ref, ga_ref, sma_ref, gp_ref, gate_ref, wbra_ref, wout_ref, y_ref):
    h_a = (o_ref[...].astype(jnp.float32) * ga_ref[...].astype(jnp.float32)).astype(jnp.bfloat16)
    br_a = _dot(h_a, wbra_ref[...])
    merged = sma_ref[...].astype(jnp.float32) * br_a + gp_ref[...].astype(jnp.float32)
    y_ref[...] = x_ref[...] + gate_ref[...] * _dot(merged.astype(jnp.bfloat16), wout_ref[...])


def _out_proj(x, o, ga, sma, gp, gate, w_br_a, w_out, *, layer, tm, tiles_per_seq):
    rows, d = x.shape

    def of_layer(a):
        return pl.BlockSpec((None,) + a.shape[1:], lambda i: (layer, 0, 0))

    def rowblk(width):
        return pl.BlockSpec((tm, width), lambda i: (i, 0))

    if gate.ndim == 3:
        gate_spec = pl.BlockSpec((None, 1, d), lambda i: (i // tiles_per_seq, 0, 0))
    else:
        gate_spec = pl.BlockSpec((tm, d), lambda i: (i, 0))
    return pl.pallas_call(
        _out_kernel,
        grid=(rows // tm,),
        in_specs=[rowblk(d), rowblk(D_ATTN), rowblk(D_ATTN), rowblk(d), rowblk(d), gate_spec,
                  of_layer(w_br_a), of_layer(w_out)],
        out_specs=rowblk(d),
        out_shape=jax.ShapeDtypeStruct((rows, d), jnp.float32),
        compiler_params=pltpu.CompilerParams(dimension_semantics=("arbitrary",), vmem_limit_bytes=VMEM_LIMIT),
        name="out_proj",
    )(x, o, ga, sma, gp, gate, w_br_a, w_out)


def kernel(x_prompt, x_sample, cache_k, cache_v, cache_logf, state_pool, page_table, c_prompt, c_sample,
           norm_g, w_ada, b_ada, w_in, b_f, q_norm_g, k_norm_g, w_pool_grp, pool_scale, w_br_a, w_br_p, w_out):
    f32, bf = jnp.float32, jnp.bfloat16
    b, t, d = x_prompt.shape
    db, s_new, _ = x_sample.shape
    depth = w_in.shape[0]
    page = cache_k.shape[2]
    n_pages = page_table.shape[1]
    tm = 512
    tq = 512
    assert t % tm == 0 and t % tq == 0 and db % 8 == 0 and s_new <= Q_ROWS and n_pages == page_table.shape[1]

    o_q, o_k, o_v = 0, D_ATTN, 2 * D_ATTN
    o_f = 3 * D_ATTN
    o_ga = o_f + N_HEADS
    o_z = o_ga + D_ATTN
    o_gp = o_z + D_POOL
    o_m = o_gp + D_POOL
    lw = {
        "w_qkv": w_in[:, :, o_q:o_f].astype(bf),
        "w_rest": w_in[:, :, o_ga:].astype(bf),
        "w_f": jnp.pad(w_in[:, :, o_f:o_ga], ((0, 0), (0, 0), (0, F_PAD - N_HEADS))).astype(bf),
        "b_f": jnp.pad(b_f, ((0, 0), (0, F_PAD - N_HEADS))).reshape(depth, 1, F_PAD),
        "norm_g": norm_g.reshape(depth, 1, d),
        "q_gain": jnp.tile(q_norm_g, (1, N_HEADS)).reshape(depth, 1, D_ATTN),
        "k_gain": jnp.tile(k_norm_g, (1, N_HEADS)).reshape(depth, 1, D_ATTN),
        "w_pool": w_pool_grp.astype(bf),
        "pool_scale": pool_scale.reshape(depth, 1, D_POOL),
        "w_br_p": w_br_p.astype(bf),
    }
    w_br_a_bf, w_out_bf = w_br_a.astype(bf), w_out.astype(bf)

    rp =lax.broadcasted_iota(jnp.int32, (page, page), 0)
    cp = lax.broadcasted_iota(jnp.int32, (page, page), 1)
    lower = (rp > cp).astype(bf)
    cache_kt = jnp.transpose(cache_k, (0, 1, 3, 4, 2))
    cache_vt = jnp.transpose(cache_v, (0, 1, 3, 4, 2))
    logf_t = jnp.transpose(cache_logf, (0, 1, 3, 2))

    mc = db + 8
    c_all = jnp.concatenate([c_sample, c_prompt, jnp.zeros((mc - db - b, d), f32)], axis=0)
    mod = _modulation(c_all, w_ada, b_ada)

    hp = x_prompt.reshape(b * t, d)
    hs = jnp.transpose(x_sample, (1, 0, 2)).reshape(s_new * db, d)
    hist_zero = jnp.zeros((POOL_SLOTS, D_POOL), f32)
    tiles_per_seq = t // tm

    outs = {n: [] for n in ("kp", "vp", "fp", "pp", "ks", "vs", "fs", "ps")}
    for l in range(depth):
        mod_s = mod[l, :db]
        mod_p = mod[l, db:db + b]
        shift_p, scale_p, gate_p = (mod_p[:, j * d:(j + 1) * d].reshape(b, 1, d) for j in range(3))
        mod_s_rows = jnp.tile(mod_s, (s_new, 1))
        shift_s, scale_s, gate_s = (mod_s_rows[:, j * d:(j + 1) * d] for j in range(3))

        (q_p, k_stack, kb_p, v_stack, vb_p, lf_p, ga_p, sma_p, gp_p, tail_p) = _inproj(
            hp, scale_p, shift_p, hist_zero, lw, layer=l, tm=tm, tstride=1, tiles_per_seq=tiles_per_seq,
            prompt=True, depth=depth, kv_bufs=() if l == 0 else (k_stack, v_stack))
        w_aug = N_HEADS * LANES
        o_p = _prompt_attention(q_p.reshape(b, t, w_aug), kb_p.reshape(b, t, w_aug), vb_p, tq=tq, hg=4)
        hp = _out_proj(hp, o_p.reshape(b * t, D_ATTN), ga_p, sma_p, gp_p, gate_p, w_br_a_bf, w_out_bf,
                       layer=l, tm=tm, tiles_per_seq=tiles_per_seq)
        outs["fp"].append(lf_p.reshape(b, t, N_HEADS))
        outs["pp"].append(tail_p[:, 1:, :])

        hist_s = jnp.concatenate([jnp.zeros((1, db, D_POOL), f32), jnp.transpose(state_pool[l], (1, 0, 2))],
                                 axis=0).reshape(POOL_SLOTS * db, D_POOL)
        (q_s, k32_s, kb_s, v32_s, vb_s, lf_s, ga_s, sma_s, gp_s, tail_s) = _inproj(
            hs, scale_s, shift_s, hist_s, lw, layer=l, tm=s_new * db, tstride=db, tiles_per_seq=1,
            prompt=False)
        q16 = jnp.transpose(q_s.reshape(s_new, db, N_HEADS, HEAD_DIM), (1, 2, 0, 3))
        q16 = jnp.pad(q16, ((0, 0), (0, 0), (0, Q_ROWS - s_new), (0, 0)))
        kn_t = jnp.pad(jnp.transpose(kb_s.reshape(s_new, db, D_ATTN), (1, 0, 2)),
                       ((0, 0), (0, Q_ROWS - s_new), (0, 0)))
        vn_t = jnp.pad(jnp.transpose(vb_s.reshape(s_new, db, D_ATTN), (1, 0, 2)),
                       ((0, 0), (0, Q_ROWS - s_new), (0, 0)))
        lf_bt = jnp.transpose(lf_s.reshape(s_new, db, N_HEADS), (1, 0, 2))
        lf_bh = jnp.transpose(lf_bt, (0, 2, 1))
        lf_new = jnp.pad(lf_bh, ((0, 0), (0, 0), (0, page - s_new)))
        o_s = _sample_attention(page_table, q16, kn_t, vn_t, lf_new, lower, logf_t, cache_kt, cache_vt, l,
                                s_new=s_new)
        o_s = jnp.transpose(o_s[:, :, :s_new, :], (2, 0, 1, 3)).reshape(s_new * db, D_ATTN).astype(bf)
        hs = _out_proj(hs, o_s, ga_s, sma_s, gp_s, gate_s, w_br_a_bf, w_out_bf, layer=l, tm=s_new * db,
                       tiles_per_seq=1)
        outs["ks"].append(jnp.transpose(k32_s.reshape(s_new, db, N_HEADS, HEAD_DIM), (1, 0, 2, 3)))
        outs["vs"].append(jnp.transpose(v32_s.reshape(s_new, db, N_HEADS, HEAD_DIM), (1, 0, 2, 3)))
        outs["fs"].append(lf_bt)
        outs["ps"].append(jnp.transpose(tail_s.reshape(POOL_SLOTS, db, D_POOL)[1:], (1, 0, 2)))

    y_p = hp.reshape(b, t, d)
    y_s = jnp.transpose(hs.reshape(s_new, db, d), (1, 0, 2))
    k_p = jnp.transpose(k_stack.reshape(depth, b, N_HEADS, HEAD_DIM, t), (0, 1, 4, 2, 3))
    v_p = jnp.transpose(v_stack.reshape(depth, b, N_HEADS, HEAD_DIM, t), (0, 1, 4, 2, 3))
    return (y_p, y_s, k_p, v_p, jnp.stack(outs["fp"]),
            jnp.stack(outs["pp"]), jnp.stack(outs["ks"]), jnp.stack(outs["vs"]), jnp.stack(outs["fs"]),
            jnp.stack(outs["ps"]))
```
